```python
import jax, jax.numpy as jnp
from jax import lax
import numpy as np

D_MODEL = 1024
BATCH = 8
SEQ = 4096
DEPTH = 4

N_MIXERS = 3
N_CONV = (DEPTH + 2) // 3
N_ATTN = (DEPTH + 1) // 3
N_POOL = DEPTH // 3

CONV_WIDTH = 31

ATTN_GROUPS = ((128, 1), (512, 4), (2048, 16))
N_ATTN_GROUPS = len(ATTN_GROUPS)
HEADS_PER_GROUP = 8
HEAD_DIM = 64
ATTN_GROUP_WIDTH = HEADS_PER_GROUP * HEAD_DIM
ATTN_QKV_WIDTH = 3 * N_ATTN_GROUPS * ATTN_GROUP_WIDTH
ROPE_THETA = 10000.0
NEG_INF = -1e30

POOL_WINDOWS = (2, 4, 8, 16)
POOL_GROUPS = len(POOL_WINDOWS)
POOL_GROUP_DIM = D_MODEL // POOL_GROUPS

FFN_HIDDEN = -(-8 * D_MODEL // (3 * 256)) * 256

RMS_EPS = 1e-6
LN_EPS = 1e-5

kernel_name = "hybrid_conv_dilattn_pool_trunk"


def rmsnorm(x, g):
    xf = x.astype(jnp.float32)
    y = xf * lax.rsqrt(jnp.mean(xf * xf, axis=-1, keepdims=True) + RMS_EPS)
    return (y * g.astype(jnp.float32)).astype(x.dtype)


def layernorm(x, g, b):
    xf = x.astype(jnp.float32)
    mu = jnp.mean(xf, axis=-1, keepdims=True)
    var = jnp.mean(jnp.square(xf - mu), axis=-1, keepdims=True)
    y = (xf - mu) * lax.rsqrt(var + LN_EPS)
    return (y * g.astype(jnp.float32) + b.astype(jnp.float32)).astype(x.dtype)


def rope(x):
    S, Dh = x.shape[1], x.shape[-1]
    half = Dh // 2
    inv_freq = ROPE_THETA ** (-jnp.arange(half, dtype=jnp.float32) / half)
    ang = jnp.arange(S, dtype=jnp.float32)[:, None] * inv_freq[None, :]
    cos = jnp.cos(ang)[None, :, None, :]
    sin = jnp.sin(ang)[None, :, None, :]
    xf = x.astype(jnp.float32)
    x1, x2 = xf[..., :half], xf[..., half:]
    out = jnp.concatenate([x1 * cos - x2 * sin, x2 * cos + x1 * sin], axis=-1)
    return out.astype(x.dtype)


def conformer_conv(h, w_in, b_in, w_dw, b_dw, ln_g, ln_b, w_out, b_out):
    D = h.shape[-1]
    a, gate = jnp.split(h @ w_in + b_in, 2, axis=-1)
    u = a * jax.nn.sigmoid(gate)
    u = lax.conv_general_dilated(
        u, w_dw[:, None, :].astype(u.dtype), window_strides=(1,),
        padding=[(CONV_WIDTH - 1, 0)],
        dimension_numbers=("NWC", "WIO", "NWC"),
        feature_group_count=D) + b_dw
    u = jax.nn.silu(layernorm(u, ln_g, ln_b))
    return u @ w_out + b_out


def dilated_window_attention(q, k, v, window, dilation):
    B, S, H, Dh = q.shape
    blk = window // dilation
    span = dilation * blk
    L = -(-S // span) * span
    n = L // dilation
    nb = n // blk

    def gather_strided(t):
        t = jnp.pad(t, ((0, 0), (0, L - S), (0, 0), (0, 0))).reshape(B, n, dilation, H, Dh)
        return t.transpose(0, 2, 3, 1, 4).reshape(B, dilation, H, nb, blk, Dh)

    def with_prev_block(t):
        prev = jnp.pad(t, ((0, 0), (0, 0), (0, 0), (1, 0), (0, 0), (0, 0)))[:, :, :, :nb]
        return jnp.concatenate([prev, t], axis=4)

    qb = gather_strided(q).astype(jnp.float32)
    kk = with_prev_block(gather_strided(k)).astype(jnp.float32)
    vv = with_prev_block(gather_strided(v)).astype(jnp.float32)

    s = jnp.einsum("brhnqd,brhnkd->brhnqk", qb, kk) * (Dh ** -0.5)
    qi = jnp.arange(blk)[:, None] + blk
    kj = jnp.arange(2 * blk)[None, :]
    dist = qi - kj
    band = (dist >= 0) & (dist <= blk)
    has_prev = (jnp.arange(nb) > 0)[:, None, None] | (kj >= blk)[None]
    valid = band[None] & has_prev
    s = jnp.where(valid, s, NEG_INF)
    m = jnp.max(s, axis=-1, keepdims=True)
    p = jnp.exp(s - m)
    den = jnp.sum(p, axis=-1)
    o = jnp.einsum("brhnqk,brhnkd->brhnqd", p, vv) / den[..., None]
    lse = m[..., 0] + jnp.log(den)

    o = o.reshape(B, dilation, H, n, Dh).transpose(0, 3, 1, 2, 4).reshape(B, L, H, Dh)[:, :S]
    lse = lse.reshape(B, dilation, H, n).transpose(0, 3, 1, 2).reshape(B, L, H)[:, :S]
    return o, lse


def dilated_attention_mixer(h, w_qkv, w_o):
    B, S, _ = h.shape
    qkv = (h @ w_qkv).reshape(B, S, 3, N_ATTN_GROUPS * HEADS_PER_GROUP, HEAD_DIM)
    q = rope(qkv[:, :, 0]).reshape(B, S, N_ATTN_GROUPS, HEADS_PER_GROUP, HEAD_DIM)
    k = rope(qkv[:, :, 1]).reshape(B, S, N_ATTN_GROUPS, HEADS_PER_GROUP, HEAD_DIM)
    v = qkv[:, :, 2].reshape(B, S, N_ATTN_GROUPS, HEADS_PER_GROUP, HEAD_DIM)
    outs, lses = [], []
    for g, (window, dilation) in enumerate(ATTN_GROUPS):
        o_g, lse_g = dilated_window_attention(q[:, :, g], k[:, :, g], v[:, :, g], window, dilation)
        outs.append(o_g)
        lses.append(lse_g)
    alpha = jax.nn.softmax(jnp.stack(lses, axis=0), axis=0)
    o = jnp.sum(alpha[..., None] * jnp.stack(outs, axis=0), axis=0)
    return o.reshape(B, S, ATTN_GROUP_WIDTH).astype(h.dtype) @ w_o


def multiscale_pool_mixer(h, w_grp, scale):
    B, S, D = h.shape
    hf = h.astype(jnp.float32)
    cs = jnp.pad(jnp.cumsum(hf, axis=1), ((0, 0), (1, 0), (0, 0)))
    csg = cs.reshape(B, S + 1, POOL_GROUPS, POOL_GROUP_DIM)
    pooled = []
    for g, w in enumerate(POOL_WINDOWS):
        c = csg[:, :, g]
        lo = jnp.pad(c, ((0, 0), (w - 1, 0), (0, 0)))[:, :S]
        cnt = jnp.minimum(jnp.arange(1, S + 1), w).astype(jnp.float32)[None, :, None]
        pooled.append((c[:, 1:] - lo) / cnt)
    p = jnp.stack(pooled, axis=2) - hf.reshape(B, S, POOL_GROUPS, POOL_GROUP_DIM)
    y = jnp.einsum("bsgc,gce->bsge", p.astype(h.dtype), w_grp).reshape(B, S, D)
    return y * scale


def swiglu(h, w_gate, w_up, w_down):
    return (jax.nn.silu(h @ w_gate) * (h @ w_up)) @ w_down


def setup_inputs(seed: int = 0) -> dict:
    key = jax.random.key(seed)
    ks = jax.random.split(key, 24)
    f32 = jnp.float32
    D, F = D_MODEL, FFN_HIDDEN

    def nrm(k, shape, scale):
        return jax.random.normal(k, shape, f32) * scale

    return {
        "x": nrm(ks[0], (BATCH, SEQ, D), 1.0),
        "norm_mix_g": 1.0 + nrm(ks[1], (DEPTH, D), 0.02),
        "norm_ffn_g": 1.0 + nrm(ks[2], (DEPTH, D), 0.02),
        "final_norm_g": 1.0 + nrm(ks[3], (D,), 0.02),
        "conv_w_in": nrm(ks[4], (N_CONV, D, 2 * D), D ** -0.5),
        "conv_b_in": nrm(ks[5], (N_CONV, 2 * D), 0.01),
        "conv_w_dw": nrm(ks[6], (N_CONV, CONV_WIDTH, D), CONV_WIDTH ** -0.5),
        "conv_b_dw": nrm(ks[7], (N_CONV, D), 0.01),
        "conv_ln_g": 1.0 + nrm(ks[8], (N_CONV, D), 0.02),
        "conv_ln_b": nrm(ks[9], (N_CONV, D), 0.01),
        "conv_w_out": nrm(ks[10], (N_CONV, D, D), D ** -0.5),
        "conv_b_out": nrm(ks[11], (N_CONV, D), 0.01),
        "attn_w_qkv": nrm(ks[12], (N_ATTN, D, ATTN_QKV_WIDTH), D ** -0.5),
        "attn_w_o": nrm(ks[13], (N_ATTN, ATTN_GROUP_WIDTH, D), ATTN_GROUP_WIDTH ** -0.5),
        "pool_w": nrm(ks[14], (N_POOL, POOL_GROUPS, POOL_GROUP_DIM, POOL_GROUP_DIM), POOL_GROUP_DIM ** -0.5),
        "pool_scale": 1.0 + nrm(ks[15], (N_POOL, D), 0.1),
        "ffn_w_gate": nrm(ks[16], (DEPTH, D, F), D ** -0.5),
        "ffn_w_up": nrm(ks[17], (DEPTH, D, F), D ** -0.5),
        "ffn_w_down": nrm(ks[18], (DEPTH, F, D), F ** -0.5),
    }


def reference(x, norm_mix_g, norm_ffn_g, final_norm_g,
              conv_w_in, conv_b_in, conv_w_dw, conv_b_dw, conv_ln_g, conv_ln_b,
              conv_w_out, conv_b_out, attn_w_qkv, attn_w_o, pool_w, pool_scale,
              ffn_w_gate, ffn_w_up, ffn_w_down):
    for i in range(DEPTH):
        kind, j = i % N_MIXERS, i // N_MIXERS
        h = rmsnorm(x, norm_mix_g[i])
        if kind == 0:
            y = conformer_conv(h, conv_w_in[j], conv_b_in[j], conv_w_dw[j], conv_b_dw[j],
                               conv_ln_g[j], conv_ln_b[j], conv_w_out[j], conv_b_out[j])
        elif kind == 1:
            y = dilated_attention_mixer(h, attn_w_qkv[j], attn_w_o[j])
        else:
            y = multiscale_pool_mixer(h, pool_w[j], pool_scale[j])
        x = x + y.astype(x.dtype)
        h = rmsnorm(x, norm_ffn_g[i])
        x = x + swiglu(h, ffn_w_gate[i], ffn_w_up[i], ffn_w_down[i]).astype(x.dtype)
    return rmsnorm(x, final_norm_g)
```

```python
import functools

import jax
import jax.numpy as jnp
from jax import lax
from jax.experimental import pallas as pl
from jax.experimental.pallas import tpu as pltpu

F32 = jnp.float32
BF16 = jnp.bfloat16

RMS_EPS = 1e-6
LN_EPS = 1e-5

CONV_WIDTH = 31
CONV_HALO = 32
CONV_ROWS = 64

ATTN_GROUPS = ((128, 1), (512, 4), (2048, 16))
HEADS = 8
HEAD_DIM = 64
GROUP_WIDTH = HEADS * HEAD_DIM
ATTN_BLOCK = 128
ROPE_THETA = 10000.0
NEG_INF = -1e30

POOL_WINDOWS = (2, 4, 8, 16)
POOL_HALO = 16

LANES = 128
STAT_LANES = LANES // HEADS
VMEM_LIMIT_BYTES = 56 * 1024 * 1024

TOKEN_TILE = 512


def _const_spec(shape):
    nd = len(shape)
    return pl.BlockSpec(shape, lambda *_: (0,) * nd, pipeline_mode=pl.Buffered(1))


def _params(n_axes):
    return pltpu.CompilerParams(
        dimension_semantics=("arbitrary",) * n_axes,
        vmem_limit_bytes=VMEM_LIMIT_BYTES,
    )


def _rms(x, g):
    return x * lax.rsqrt(jnp.mean(x * x, axis=-1, keepdims=True) + RMS_EPS) * g


def _dot(a, b):
    return jnp.dot(a, b, preferred_element_type=F32)


def _silu(x):
    return x * jax.nn.sigmoid(x)


def _ffn_body(*refs, chunks, final):
    if final:
        x_ref, g_ref, wg_ref, wu_ref, wd_ref, fg_ref, o_ref = refs
    else:
        x_ref, g_ref, wg_ref, wu_ref, wd_ref, o_ref = refs
    x = x_ref[...]
    h = _rms(x, g_ref[...]).astype(BF16)
    acc = None
    for c0, cw in chunks:
        gate = _dot(h, wg_ref[:, c0:c0 + cw])
        up = _dot(h, wu_ref[:, c0:c0 + cw])
        act = (_silu(gate) * up).astype(BF16)
        part = _dot(act, wd_ref[c0:c0 + cw, :])
        acc = part if acc is None else acc + part
    y = x + acc
    if final:
        y = _rms(y, fg_ref[...])
    o_ref[...] = y


def _hidden_chunks(f):
    chunks, c0 = [], 0
    while c0 < f:
        cw = min(1024, f - c0)
        chunks.append((c0, cw))
        c0 += cw
    return tuple(chunks)


def _ffn(x2d, g, wg, wu, wd, final_g=None):
    n, d = x2d.shape
    f = wg.shape[1]
    tm = TOKEN_TILE
    final = final_g is not None
    in_specs = [
        pl.BlockSpec((tm, d), lambda i: (i, 0)),
        _const_spec((1, d)),
        _const_spec((d, f)),
        _const_spec((d, f)),
        _const_spec((f, d)),
    ]
    args = [x2d, g.reshape(1, d), wg, wu, wd]
    if final:
        in_specs.append(_const_spec((1, d)))
        args.append(final_g.reshape(1, d))
    return pl.pallas_call(
        functools.partial(_ffn_body, chunks=_hidden_chunks(f), final=final),
        grid=(n // tm,),
        in_specs=in_specs,
        out_specs=pl.BlockSpec((tm, d), lambda i: (i, 0)),
        out_shape=jax.ShapeDtypeStruct((n, d), F32),
        compiler_params=_params(1),
        name="ffn",
    )(*args)


def _conv_body(x_ref, g_ref, win_ref, bin_ref, wdw_ref, bdw_ref, lng_ref, lnb_ref,
               wout_ref, bout_ref, o_ref, ubuf, cbuf, *, tm, d):
    n_slabs = d // LANES

    @pl.when(pl.program_id(1) == 0)
    def _():
        ubuf[:, 0:CONV_HALO, :] = jnp.zeros((n_slabs, CONV_HALO, LANES), F32)

    x = x_ref[0]
    h = _rms(x, g_ref[...]).astype(BF16)
    a = _dot(h, win_ref[:, 0:d]) + bin_ref[:, 0:d]
    gate = _dot(h, win_ref[:, d:2 * d]) + bin_ref[:, d:2 * d]
    u = a * jax.nn.sigmoid(gate)
    for s in range(n_slabs):
        ubuf[s, CONV_HALO:CONV_HALO + tm, :] = u[:, s * LANES:(s + 1) * LANES]

    first = CONV_HALO - (CONV_WIDTH - 1)

    def conv_rows(c, carry):
        base = pl.multiple_of(c * CONV_ROWS, CONV_ROWS)
        for s in range(n_slabs):
            acc = jnp.broadcast_to(bdw_ref[s], (CONV_ROWS, LANES))
            for k in range(CONV_WIDTH):
                acc = acc + ubuf[s, pl.ds(base + first + k, CONV_ROWS, stride=1), :] * wdw_ref[s, k:k + 1, :]
            cbuf[s, pl.ds(base, CONV_ROWS), :] = acc
        return carry

    lax.fori_loop(0, tm // CONV_ROWS, conv_rows, 0)
    for s in range(n_slabs):
        ubuf[s, 0:CONV_HALO, :] = ubuf[s, tm:tm + CONV_HALO, :]

    cv = jnp.concatenate([cbuf[s] for s in range(n_slabs)], axis=-1)
    mu = jnp.mean(cv, axis=-1, keepdims=True)
    cen = cv - mu
    var = jnp.mean(cen * cen, axis=-1, keepdims=True)
    y = cen * lax.rsqrt(var + LN_EPS) * lng_ref[...] + lnb_ref[...]
    y = _silu(y).astype(BF16)
    o_ref[0] = x + _dot(y, wout_ref[...]) + bout_ref[...]


def _slabs(v):
    rows, d = v.shape
    return v.reshape(rows, d // LANES, LANES).transpose(1, 0, 2)


def _conv_mixer(x, g, w_in, b_in, w_dw, b_dw, ln_g, ln_b, w_out, b_out):
    b, s, d = x.shape
    tm = TOKEN_TILE
    row = lambda v: v.reshape(1, -1)
    tile = pl.BlockSpec((1, tm, d), lambda i, j: (i, j, 0))
    return pl.pallas_call(
        functools.partial(_conv_body, tm=tm, d=d),
        grid=(b, s // tm),
        in_specs=[
            tile,
            _const_spec((1, d)),
            _const_spec((d, 2 * d)),
            _const_spec((1, 2 * d)),
            _const_spec((d // LANES, CONV_WIDTH, LANES)),
            _const_spec((d // LANES, 1, LANES)),
            _const_spec((1, d)),
            _const_spec((1, d)),
            _const_spec((d, d)),
            _const_spec((1, d)),
        ],
        out_specs=tile,
        out_shape=jax.ShapeDtypeStruct((b, s, d), F32),
        scratch_shapes=[
            pltpu.VMEM((d // LANES, tm + CONV_HALO, LANES), F32),
            pltpu.VMEM((d // LANES, tm, LANES), F32),
        ],
        compiler_params=_params(2),
        name="conv_mixer",
    )(x, row(g), w_in, row(b_in), _slabs(w_dw), _slabs(row(b_dw)), row(ln_g), row(ln_b), w_out, row(b_out))


def _pool_body(x_ref, g_ref, w_ref, sc_ref, o_ref, hbuf, *, tm, d):
    j = pl.program_id(1)

    n_slabs = d // LANES

    @pl.when(j == 0)
    def _():
        hbuf[:, 0:POOL_HALO, :] = jnp.zeros((n_slabs, POOL_HALO, LANES), F32)

    x = x_ref[0]
    h = _rms(x, g_ref[...])
    for s in range(n_slabs):
        hbuf[s, POOL_HALO:POOL_HALO + tm, :] = h[:, s * LANES:(s + 1) * LANES]
    pos = j * tm + lax.broadcasted_iota(jnp.int32, (tm, 1), 0) + 1
    per_group = n_slabs // len(POOL_WINDOWS)
    outs = []
    for gi, w in enumerate(POOL_WINDOWS):
        cnt = jnp.minimum(pos, w).astype(F32)
        cols = []
        for s in range(gi * per_group, (gi + 1) * per_group):
            cur = h[:, s * LANES:(s + 1) * LANES]
            tot = cur
            for k in range(1, w):
                tot = tot + hbuf[s, pl.ds(POOL_HALO - k, tm, stride=1), :]
            cols.append(tot / cnt - cur)
        p = jnp.concatenate(cols, axis=-1)
        outs.append(_dot(p.astype(BF16), w_ref[gi]))
    for s in range(n_slabs):
        hbuf[s, 0:POOL_HALO, :] = hbuf[s, tm:tm + POOL_HALO, :]
    o_ref[0] = x + jnp.concatenate(outs, axis=-1) * sc_ref[...]


def _pool_mixer(x, g, w_grp, scale):
    b, s, d = x.shape
    tm = TOKEN_TILE
    tile = pl.BlockSpec((1, tm, d), lambda i, j: (i, j, 0))
    return pl.pallas_call(
        functools.partial(_pool_body, tm=tm, d=d),
        grid=(b, s // tm),
        in_specs=[tile, _const_spec((1, d)), _const_spec(w_grp.shape), _const_spec((1, d))],
        out_specs=tile,
        out_shape=jax.ShapeDtypeStruct((b, s, d), F32),
        scratch_shapes=[pltpu.VMEM((d // LANES, tm + POOL_HALO, LANES), F32)],
        compiler_params=_params(2),
        name="pool_mixer",
    )(x, g.reshape(1, d), w_grp, scale.reshape(1, d))


def _qkv_body(x_ref, g_ref, w_ref, cos_ref, sin_ref, *rest, tm):
    outs, slab = rest[:-1], rest[-1]
    h = _rms(x_ref[0], g_ref[...]).astype(BF16)
    cos = cos_ref[...]
    sin = sin_ref[...]
    lane = lax.broadcasted_iota(jnp.int32, (tm, LANES), 1)
    first_half = (lane % HEAD_DIM) < HEAD_DIM // 2
    n_slabs = GROUP_WIDTH // LANES
    for kind in range(3):
        for gi, (_, dil) in enumerate(ATTN_GROUPS):
            c0 = (kind * len(ATTN_GROUPS) + gi) * GROUP_WIDTH
            y = _dot(h, w_ref[:, c0:c0 + GROUP_WIDTH])
            out = outs[kind * len(ATTN_GROUPS) + gi]
            for s in range(n_slabs):
                ys = y[:, s * LANES:(s + 1) * LANES]
                if kind < 2:
                    half = HEAD_DIM // 2
                    swapped = jnp.where(first_half,
                                        pltpu.roll(ys, LANES - half, axis=1),
                                        pltpu.roll(ys, half, axis=1))
                    ys = ys * cos + swapped * sin
                if kind == 0:
                    ys = ys * (HEAD_DIM ** -0.5)
                if dil == 1:
                    out[0, :, s * LANES:(s + 1) * LANES] = ys.astype(BF16)
                else:
                    slab[s] = ys
            if dil > 1:
                rows = tm // dil
                for r in range(dil):
                    for s in range(n_slabs):
                        lo = r * GROUP_WIDTH + s * LANES
                        out[0, :, lo:lo + LANES] = slab[s, pl.ds(r, rows, stride=dil), :].astype(BF16)


def _rope_tables(s):
    half = HEAD_DIM // 2
    inv_freq = ROPE_THETA ** (-jnp.arange(half, dtype=F32) / half)
    ang = jnp.arange(s, dtype=F32)[:, None] * inv_freq[None, :]
    cos = jnp.tile(jnp.cos(ang), (1, LANES // half))
    sin = jnp.sin(ang)
    sin = jnp.tile(jnp.concatenate([-sin, sin], axis=-1), (1, LANES // HEAD_DIM))
    return cos, sin


def _qkv_rope(x, g, w_qkv):
    b, s, d = x.shape
    tm = TOKEN_TILE
    cos, sin = _rope_tables(s)
    out_specs, out_shapes = [], []
    for _ in range(3):
        for _, dil in ATTN_GROUPS:
            out_specs.append(pl.BlockSpec((1, tm // dil, dil * GROUP_WIDTH), lambda j, i: (i, j, 0)))
            out_shapes.append(jax.ShapeDtypeStruct((b, s // dil, dil * GROUP_WIDTH), BF16))
    return pl.pallas_call(
        functools.partial(_qkv_body, tm=tm),
        grid=(s // tm, b),
        in_specs=[
            pl.BlockSpec((1, tm, d), lambda j, i: (i, j, 0)),
            _const_spec((1, d)),
            _const_spec(w_qkv.shape),
            pl.BlockSpec((tm, LANES), lambda j, i: (j, 0)),
            pl.BlockSpec((tm, LANES), lambda j, i: (j, 0)),
        ],
        out_specs=out_specs,
        out_shape=out_shapes,
        scratch_shapes=[pltpu.VMEM((GROUP_WIDTH // LANES, tm, LANES), F32)],
        compiler_params=_params(2),
        name="qkv_rope",
    )(x, g.reshape(1, d), w_qkv, cos, sin)


def _attn_body(q_ref, k_ref, v_ref, o_ref, st_ref, *, n):
    blk = ATTN_BLOCK
    lane = lax.broadcasted_iota(jnp.int32, (blk, LANES), 1)
    head0 = lane < HEAD_DIM
    stat_head = lane // STAT_LANES
    qi = lax.broadcasted_iota(jnp.int32, (2 * blk, 2 * blk), 0) % blk
    kj = lax.broadcasted_iota(jnp.int32, (2 * blk, 2 * blk), 1)
    band = (kj >= qi) & (kj <= qi + blk)

    def one_block(ib, carry):
        row0 = pl.multiple_of(ib * blk, blk)
        prow0 = pl.multiple_of(jnp.maximum(ib - 1, 0) * blk, blk)
        valid = band & ((kj >= blk) | (ib > 0))
        stats = jnp.zeros((blk, LANES), F32)
        for p in range(GROUP_WIDTH // LANES):
            lanes = slice(p * LANES, (p + 1) * LANES)
            q2 = q_ref[0, pl.ds(row0, blk), lanes]
            k2 = jnp.concatenate([k_ref[0, pl.ds(prow0, blk), lanes],
                                  k_ref[0, pl.ds(row0, blk), lanes]], axis=0)
            v2 = jnp.concatenate([v_ref[0, pl.ds(prow0, blk), lanes],
                                  v_ref[0, pl.ds(row0, blk), lanes]], axis=0)
            zero = jnp.zeros_like(q2)
            qq = jnp.concatenate([jnp.where(head0, q2, zero), jnp.where(head0, zero, q2)], axis=0)
            sc = lax.dot_general(qq, k2, (((1,), (1,)), ((), ())), preferred_element_type=F32)
            sc = jnp.where(valid, sc, NEG_INF)
            m = jnp.max(sc, axis=-1, keepdims=True)
            pe = jnp.exp(sc - m)
            den = jnp.sum(pe, axis=-1, keepdims=True)
            pv = _dot(pe.astype(BF16), v2)
            pv = pv * (1.0 / den)
            o_ref[0, pl.ds(row0, blk), lanes] = jnp.where(head0, pv[:blk], pv[blk:]).astype(BF16)
            lse = m + jnp.log(den)
            stats = jnp.where(stat_head == 2 * p, lse[:blk], stats)
            stats = jnp.where(stat_head == 2 * p + 1, lse[blk:], stats)
        st_ref[0, 0, pl.ds(row0, blk), :] = stats
        return carry

    lax.fori_loop(0, n // blk, one_block, 0)


def _attn_group(q, k, v, dil):
    b, n, _ = q.shape
    seq = pl.BlockSpec((1, n, GROUP_WIDTH), lambda i, r: (i, 0, r))
    return pl.pallas_call(
        functools.partial(_attn_body, n=n),
        grid=(b, dil),
        in_specs=[seq, seq, seq],
        out_specs=[seq, pl.BlockSpec((1, 1, n, LANES), lambda i, r: (i, r, 0, 0))],
        out_shape=[jax.ShapeDtypeStruct((b, n, dil * GROUP_WIDTH), BF16),
                   jax.ShapeDtypeStruct((b, dil, n, LANES), F32)],
        compiler_params=_params(2),
        name=f"attn_dil{dil}",
    )(q, k, v)


def _merge_body(x_ref, *rest, tm):
    ng = len(ATTN_GROUPS)
    o_refs, st_refs = rest[:ng], rest[ng:2 * ng]
    wo_ref, ex_ref, out_ref, oscr, sscr = rest[2 * ng:]
    n_slabs = GROUP_WIDTH // LANES
    vals, lses = [], []
    for gi, (_, dil) in enumerate(ATTN_GROUPS):
        if dil == 1:
            vals.append(o_refs[gi][0].astype(F32))
            lses.append(st_refs[gi][0, 0])
            continue
        rows = tm // dil
        for r in range(dil):
            for s in range(n_slabs):
                lo = r * GROUP_WIDTH + s * LANES
                oscr[gi, s, pl.ds(r, rows, stride=dil), :] = o_refs[gi][0, :, lo:lo + LANES].astype(F32)
            sscr[gi, pl.ds(r, rows, stride=dil), :] = st_refs[gi][0, r]
        vals.append(jnp.concatenate([oscr[gi, s] for s in range(n_slabs)], axis=-1))
        lses.append(sscr[gi])
    top = functools.reduce(jnp.maximum, lses)
    es = [jnp.exp(l - top) for l in lses]
    inv = 1.0 / functools.reduce(lambda a, c: a + c, es)
    o = None
    for e, val in zip(es, vals):
        alpha = e * inv
        hi = alpha.astype(BF16)
        lo = (alpha - hi.astype(F32)).astype(BF16)
        wide = _dot(hi, ex_ref[...]) + _dot(lo, ex_ref[...])
        o = wide * val if o is None else o + wide * val
    out_ref[0] = x_ref[0] + _dot(o.astype(BF16), wo_ref[...])


def _merge_groups(x, os_, sts, w_o):
    b, s, d = x.shape
    tm = TOKEN_TILE
    ng = len(ATTN_GROUPS)
    src = jnp.arange(LANES)[:, None]
    dst = jnp.arange(GROUP_WIDTH)[None, :]
    expand = (src == (dst // HEAD_DIM) * STAT_LANES).astype(BF16)
    tile = pl.BlockSpec((1, tm, d), lambda i, j: (i, j, 0))
    in_specs = [tile]
    for _, dil in ATTN_GROUPS:
        in_specs.append(pl.BlockSpec((1, tm // dil, dil * GROUP_WIDTH), lambda i, j: (i, j, 0)))
    for _, dil in ATTN_GROUPS:
        in_specs.append(pl.BlockSpec((1, dil, tm // dil, LANES), lambda i, j: (i, 0, j, 0)))
    in_specs += [_const_spec(w_o.shape), _const_spec(expand.shape)]
    return pl.pallas_call(
        functools.partial(_merge_body, tm=tm),
        grid=(b, s // tm),
        in_specs=in_specs,
        out_specs=tile,
        out_shape=jax.ShapeDtypeStruct((b, s, d), F32),
        scratch_shapes=[pltpu.VMEM((ng, GROUP_WIDTH // LANES, tm, LANES), F32),
                        pltpu.VMEM((ng, tm, LANES), F32)],
        compiler_params=_params(2),
        name="attn_merge",
    )(x, *os_, *sts, w_o, expand)


def _attn_mixer(x, g, w_qkv, w_o):
    ng = len(ATTN_GROUPS)
    qkv = _qkv_rope(x, g, w_qkv)
    os_, sts = [], []
    for gi, (_, dil) in enumerate(ATTN_GROUPS):
        o, st = _attn_group(qkv[gi], qkv[ng + gi], qkv[2 * ng + gi], dil)
        os_.append(o)
        sts.append(st)
    return _merge_groups(x, os_, sts, w_o)


def kernel(x, norm_mix_g, norm_ffn_g, final_norm_g, conv_w_in, conv_b_in, conv_w_dw, conv_b_dw,
           conv_ln_g, conv_ln_b, conv_w_out, conv_b_out, attn_w_qkv, attn_w_o, pool_w, pool_scale,
           ffn_w_gate, ffn_w_up, ffn_w_down):
    b, s, d = x.shape
    depth = norm_mix_g.shape[0]
    bf = lambda w: w.astype(BF16)
    for i in range(depth):
        kind, j = i % 3, i // 3
        if kind == 0:
            x = _conv_mixer(x, norm_mix_g[i], bf(conv_w_in[j]), conv_b_in[j], conv_w_dw[j], conv_b_dw[j],
                            conv_ln_g[j], conv_ln_b[j], bf(conv_w_out[j]), conv_b_out[j])
        elif kind == 1:
            x = _attn_mixer(x, norm_mix_g[i], bf(attn_w_qkv[j]), bf(attn_w_o[j]))
        else:
            x = _pool_mixer(x, norm_mix_g[i], bf(pool_w[j]), pool_scale[j])
        last = i == depth - 1
        x = _ffn(x.reshape(b * s, d), norm_ffn_g[i], bf(ffn_w_gate[i]), bf(ffn_w_up[i]), bf(ffn_w_down[i]),
                 final_g=final_norm_g if last else None).reshape(b, s, d)
    return x
```

```python
import functools

import jax
import jax.numpy as jnp
from jax import lax
from jax.experimental import pallas as pl
from jax.experimental.pallas import tpu as pltpu

F32 = jnp.float32
BF16 = jnp.bfloat16

RMS_EPS = 1e-6
LN_EPS = 1e-5

CONV_WIDTH = 31
CONV_HALO = 32
CONV_ROWS = 64

ATTN_GROUPS = ((128, 1), (512, 4), (2048, 16))
HEADS = 8
HEAD_DIM = 64
GROUP_WIDTH = HEADS * HEAD_DIM
ATTN_BLOCK = 128
ATTN_UNROLL = 4
LOG2_E = 1.4426950408889634
Q_SCALE = HEAD_DIM ** -0.5 * LOG2_E
ROPE_THETA = 10000.0
NEG_INF = -1e30

POOL_WINDOWS = (2, 4, 8, 16)
POOL_HALO = 16

LANES = 128
STAT_LANES = LANES // HEADS
VMEM_LIMIT_BYTES = 56 * 1024 * 1024

TOKEN_TILE = 512
FFN_TILE = 1024
FFN_CHUNK = 512


def _const_spec(shape):
    nd = len(shape)
    return pl.BlockSpec(shape, lambda *_: (0,) * nd, pipeline_mode=pl.Buffered(1))


def _params(n_axes):
    return pltpu.CompilerParams(
        dimension_semantics=("arbitrary",) * n_axes,
        vmem_limit_bytes=VMEM_LIMIT_BYTES,
    )


def _rms(x, g):
    return x * lax.rsqrt(jnp.mean(x * x, axis=-1, keepdims=True) + RMS_EPS) * g


def _dot(a, b):
    return jnp.dot(a, b, preferred_element_type=F32)


def _silu(x):
    return x * jax.nn.sigmoid(x)


def _ffn_body(*refs, chunks, final):
    if final:
        x_ref, g_ref, wg_ref, wu_ref, wd_ref, fg_ref, o_ref = refs
    else:
        x_ref, g_ref, wg_ref, wu_ref, wd_ref, o_ref = refs
    x = x_ref[...]
    h = _rms(x, g_ref[...]).astype(BF16)
    acc = None
    for c0, cw in chunks:
        gate = _dot(h, wg_ref[:, c0:c0 + cw])
        up = _dot(h, wu_ref[:, c0:c0 + cw])
        act = (_silu(gate) * up).astype(BF16)
        part = _dot(act, wd_ref[c0:c0 + cw, :])
        acc = part if acc is None else acc + part
    y = x + acc
    if final:
        y = _rms(y, fg_ref[...])
    o_ref[...] = y


def _hidden_chunks(f):
    chunks, c0 = [], 0
    while c0 < f:
        cw = min(FFN_CHUNK, f - c0)
        chunks.append((c0, cw))
        c0 += cw
    return tuple(chunks)


def _ffn(x2d, g, wg, wu, wd, final_g=None):
    n, d = x2d.shape
    f = wg.shape[1]
    tm = FFN_TILE
    final = final_g is not None
    in_specs = [
        pl.BlockSpec((tm, d), lambda i: (i, 0)),
        _const_spec((1, d)),
        _const_spec((d, f)),
        _const_spec((d, f)),
        _const_spec((f, d)),
    ]
    args = [x2d, g.reshape(1, d), wg, wu, wd]
    if final:
        in_specs.append(_const_spec((1, d)))
        args.append(final_g.reshape(1, d))
    return pl.pallas_call(
        functools.partial(_ffn_body, chunks=_hidden_chunks(f), final=final),
        grid=(n // tm,),
        in_specs=in_specs,
        out_specs=pl.BlockSpec((tm, d), lambda i: (i, 0)),
        out_shape=jax.ShapeDtypeStruct((n, d), F32),
        compiler_params=_params(1),
        name="ffn",
    )(*args)


def _conv_body(x_ref, g_ref, win_ref, bin_ref, wdw_ref, bdw_ref, lng_ref, lnb_ref,
               wout_ref, bout_ref, o_ref, ubuf, cbuf, *, tm, d):
    n_slabs = d // LANES

    @pl.when(pl.program_id(1) == 0)
    def _():
        ubuf[:, 0:CONV_HALO, :] = jnp.zeros((n_slabs, CONV_HALO, LANES), F32)

    x = x_ref[0]
    h = _rms(x, g_ref[...]).astype(BF16)
    a = _dot(h, win_ref[:, 0:d]) + bin_ref[:, 0:d]
    gate = _dot(h, win_ref[:, d:2 * d]) + bin_ref[:, d:2 * d]
    u = a * jax.nn.sigmoid(gate)
    for s in range(n_slabs):
        ubuf[s, CONV_HALO:CONV_HALO + tm, :] = u[:, s * LANES:(s + 1) * LANES]

    first = CONV_HALO - (CONV_WIDTH - 1)

    def conv_rows(c, carry):
        base = pl.multiple_of(c * CONV_ROWS, CONV_ROWS)
        for s in range(n_slabs):
            acc = jnp.broadcast_to(bdw_ref[s], (CONV_ROWS, LANES))
            for k in range(CONV_WIDTH):
                acc = acc + ubuf[s, pl.ds(base + first + k, CONV_ROWS, stride=1), :] * wdw_ref[s, k:k + 1, :]
            cbuf[s, pl.ds(base, CONV_ROWS), :] = acc
        return carry

    lax.fori_loop(0, tm // CONV_ROWS, conv_rows, 0)
    for s in range(n_slabs):
        ubuf[s, 0:CONV_HALO, :] = ubuf[s, tm:tm + CONV_HALO, :]

    cv = jnp.concatenate([cbuf[s] for s in range(n_slabs)], axis=-1)
    mu = jnp.mean(cv, axis=-1, keepdims=True)
    cen = cv - mu
    var = jnp.mean(cen * cen, axis=-1, keepdims=True)
    y = cen * lax.rsqrt(var + LN_EPS) * lng_ref[...] + lnb_ref[...]
    y = _silu(y).astype(BF16)
    o_ref[0] = x + _dot(y, wout_ref[...]) + bout_ref[...]


def _slabs(v):
    rows, d = v.shape
    return v.reshape(rows, d // LANES, LANES).transpose(1, 0, 2)


def _conv_mixer(x, g, w_in, b_in, w_dw, b_dw, ln_g, ln_b, w_out, b_out):
    b, s, d = x.shape
    tm = TOKEN_TILE
    row = lambda v: v.reshape(1, -1)
    tile = pl.BlockSpec((1, tm, d), lambda i, j: (i, j, 0))
    return pl.pallas_call(
        functools.partial(_conv_body, tm=tm, d=d),
        grid=(b, s // tm),
        in_specs=[
            tile,
            _const_spec((1, d)),
            _const_spec((d, 2 * d)),
            _const_spec((1, 2 * d)),
            _const_spec((d // LANES, CONV_WIDTH, LANES)),
            _const_spec((d // LANES, 1, LANES)),
            _const_spec((1, d)),
            _const_spec((1, d)),
            _const_spec((d, d)),
            _const_spec((1, d)),
        ],
        out_specs=tile,
        out_shape=jax.ShapeDtypeStruct((b, s, d), F32),
        scratch_shapes=[
            pltpu.VMEM((d // LANES, tm + CONV_HALO, LANES), F32),
            pltpu.VMEM((d // LANES, tm, LANES), F32),
        ],
        compiler_params=_params(2),
        name="conv_mixer",
    )(x, row(g), w_in, row(b_in), _slabs(w_dw), _slabs(row(b_dw)), row(ln_g), row(ln_b), w_out, row(b_out))


def _pool_body(x_ref, g_ref, w_ref, sc_ref, o_ref, hbuf, *, tm, d):
    j = pl.program_id(1)

    n_slabs = d // LANES

    @pl.when(j == 0)
    def _():
        hbuf[:, 0:POOL_HALO, :] = jnp.zeros((n_slabs, POOL_HALO, LANES), F32)

    x = x_ref[0]
    h = _rms(x, g_ref[...])
    for s in range(n_slabs):
        hbuf[s, POOL_HALO:POOL_HALO + tm, :] = h[:, s * LANES:(s + 1) * LANES]
    pos = j * tm + lax.broadcasted_iota(jnp.int32, (tm, 1), 0) + 1
    per_group = n_slabs // len(POOL_WINDOWS)
    outs = []
    for gi, w in enumerate(POOL_WINDOWS):
        cnt = jnp.minimum(pos, w).astype(F32)
        cols = []
        for s in range(gi * per_group, (gi + 1) * per_group):
            cur = h[:, s * LANES:(s + 1) * LANES]
            tot = cur
            for k in range(1, w):
                tot = tot + hbuf[s, pl.ds(POOL_HALO - k, tm, stride=1), :]
            cols.append(tot / cnt - cur)
        p = jnp.concatenate(cols, axis=-1)
        outs.append(_dot(p.astype(BF16), w_ref[gi]))
    for s in range(n_slabs):
        hbuf[s, 0:POOL_HALO, :] = hbuf[s, tm:tm + POOL_HALO, :]
    o_ref[0] = x + jnp.concatenate(outs, axis=-1) * sc_ref[...]


def _pool_mixer(x, g, w_grp, scale):
    b, s, d = x.shape
    tm = TOKEN_TILE
    tile = pl.BlockSpec((1, tm, d), lambda i, j: (i, j, 0))
    return pl.pallas_call(
        functools.partial(_pool_body, tm=tm, d=d),
        grid=(b, s // tm),
        in_specs=[tile, _const_spec((1, d)), _const_spec(w_grp.shape), _const_spec((1, d))],
        out_specs=tile,
        out_shape=jax.ShapeDtypeStruct((b, s, d), F32),
        scratch_shapes=[pltpu.VMEM((d // LANES, tm + POOL_HALO, LANES), F32)],
        compiler_params=_params(2),
        name="pool_mixer",
    )(x, g.reshape(1, d), w_grp, scale.reshape(1, d))


def _qkv_body(x_ref, g_ref, w_ref, cos_ref, sin_ref, *rest, tm):
    outs, slab = rest[:-1], rest[-1]
    h = _rms(x_ref[0], g_ref[...]).astype(BF16)
    cos = cos_ref[...]
    sin = sin_ref[...]
    lane = lax.broadcasted_iota(jnp.int32, (tm, LANES), 1)
    first_half = (lane % HEAD_DIM) < HEAD_DIM // 2
    n_slabs = GROUP_WIDTH // LANES
    for kind in range(3):
        for gi, (_, dil) in enumerate(ATTN_GROUPS):
            c0 = (kind * len(ATTN_GROUPS) + gi) * GROUP_WIDTH
            y = _dot(h, w_ref[:, c0:c0 + GROUP_WIDTH])
            out = outs[kind * len(ATTN_GROUPS) + gi]
            for s in range(n_slabs):
                ys = y[:, s * LANES:(s + 1) * LANES]
                if kind < 2:
                    half = HEAD_DIM // 2
                    swapped = jnp.where(first_half,
                                        pltpu.roll(ys, LANES - half, axis=1),
                                        pltpu.roll(ys, half, axis=1))
                    ys = ys * cos + swapped * sin
                if kind == 0:
                    ys = ys * Q_SCALE
                if dil == 1:
                    out[0, :, s * LANES:(s + 1) * LANES] = ys.astype(BF16)
                else:
                    slab[s] = ys
            if dil > 1:
                rows = tm // dil
                for r in range(dil):
                    for s in range(n_slabs):
                        lo = r * GROUP_WIDTH + s * LANES
                        out[0, :, lo:lo + LANES] = slab[s, pl.ds(r, rows, stride=dil), :].astype(BF16)


def _rope_tables(s):
    half = HEAD_DIM // 2
    inv_freq = ROPE_THETA ** (-jnp.arange(half, dtype=F32) / half)
    ang = jnp.arange(s, dtype=F32)[:, None] * inv_freq[None, :]
    cos = jnp.tile(jnp.cos(ang), (1, LANES // half))
    sin = jnp.sin(ang)
    sin = jnp.tile(jnp.concatenate([-sin, sin], axis=-1), (1, LANES // HEAD_DIM))
    return cos, sin


def _qkv_rope(x, g, w_qkv):
    b, s, d = x.shape
    tm = TOKEN_TILE
    cos, sin = _rope_tables(s)
    out_specs, out_shapes = [], []
    for _ in range(3):
        for _, dil in ATTN_GROUPS:
            out_specs.append(pl.BlockSpec((1, tm // dil, dil * GROUP_WIDTH), lambda j, i: (i, j, 0)))
            out_shapes.append(jax.ShapeDtypeStruct((b, s // dil, dil * GROUP_WIDTH), BF16))
    return pl.pallas_call(
        functools.partial(_qkv_body, tm=tm),
        grid=(s // tm, b),
        in_specs=[
            pl.BlockSpec((1, tm, d), lambda j, i: (i, j, 0)),
            _const_spec((1, d)),
            _const_spec(w_qkv.shape),
            pl.BlockSpec((tm, LANES), lambda j, i: (j, 0)),
            pl.BlockSpec((tm, LANES), lambda j, i: (j, 0)),
        ],
        out_specs=out_specs,
        out_shape=out_shapes,
        scratch_shapes=[pltpu.VMEM((GROUP_WIDTH // LANES, tm, LANES), F32)],
        compiler_params=_params(2),
        name="qkv_rope",
    )(x, g.reshape(1, d), w_qkv, cos, sin)


def _attn_body(q_ref, k_ref, v_ref, bias_ref, o_ref, st_ref, *, n):
    blk = ATTN_BLOCK
    lane = lax.broadcasted_iota(jnp.int32, (blk, LANES), 1)
    head0 = lane < HEAD_DIM
    stat_head = lane // STAT_LANES
    ones = jnp.ones((2 * blk, LANES), BF16)

    def one_block(ib, carry):
        row0 = pl.multiple_of(ib * blk, blk)
        prow0 = pl.multiple_of(jnp.maximum(ib - 1, 0) * blk, blk)
        bias = bias_ref[jnp.minimum(ib, 1)]
        stats = jnp.zeros((blk, LANES), F32)
        for p in range(GROUP_WIDTH // LANES):
            lanes = slice(p * LANES, (p + 1) * LANES)
            q2 = q_ref[0, pl.ds(row0, blk), lanes]
            k2 = jnp.concatenate([k_ref[0, pl.ds(prow0, blk), lanes],
                                  k_ref[0, pl.ds(row0, blk), lanes]], axis=0)
            v2 = jnp.concatenate([v_ref[0, pl.ds(prow0, blk), lanes],
                                  v_ref[0, pl.ds(row0, blk), lanes]], axis=0)
            zero = jnp.zeros_like(q2)
            qq = jnp.concatenate([jnp.where(head0, q2, zero), jnp.where(head0, zero, q2)], axis=0)
            sc = lax.dot_general(qq, k2, (((1,), (1,)), ((), ())), preferred_element_type=F32) + bias
            m = jnp.max(sc, axis=-1, keepdims=True)
            pe = jnp.exp2(sc - m).astype(BF16)
            pvd = _dot(pe, jnp.concatenate([v2, ones], axis=1))
            den = pvd[:, LANES:]
            pv = pvd[:, :LANES] * (1.0 / den)
            o_ref[0, pl.ds(row0, blk), lanes] = jnp.where(head0, pv[:blk], pv[blk:]).astype(BF16)
            lse = m + jnp.log2(den)
            stats = jnp.where(stat_head == 2 * p, lse[:blk], stats)
            stats = jnp.where(stat_head == 2 * p + 1, lse[blk:], stats)
        st_ref[0, 0, pl.ds(row0, blk), :] = stats
        return carry

    lax.fori_loop(0, n // blk, one_block, 0, unroll=min(ATTN_UNROLL, n // blk))


def _attn_bias():
    blk = ATTN_BLOCK
    qi = jnp.arange(2 * blk)[:, None] % blk
    kj = jnp.arange(2 * blk)[None, :]
    band = (kj >= qi) & (kj <= qi + blk)
    first = band & (kj >= blk)
    return jnp.where(jnp.stack([first, band]), 0.0, NEG_INF).astype(F32)


def _attn_group(q, k, v, dil):
    b, n, _ = q.shape
    seq = pl.BlockSpec((1, n, GROUP_WIDTH), lambda i, r: (i, 0, r))
    bias = _attn_bias()
    return pl.pallas_call(
        functools.partial(_attn_body, n=n),
        grid=(b, dil),
        in_specs=[seq, seq, seq, _const_spec(bias.shape)],
        out_specs=[seq, pl.BlockSpec((1, 1, n, LANES), lambda i, r: (i, r, 0, 0))],
        out_shape=[jax.ShapeDtypeStruct((b, n, dil * GROUP_WIDTH), BF16),
                   jax.ShapeDtypeStruct((b, dil, n, LANES), F32)],
        compiler_params=_params(2),
        name=f"attn_dil{dil}",
    )(q, k, v, bias)


def _merge_body(x_ref, *rest, tm):
    ng = len(ATTN_GROUPS)
    o_refs, st_refs = rest[:ng], rest[ng:2 * ng]
    wo_ref, ex_ref, out_ref, oscr, sscr = rest[2 * ng:]
    n_slabs = GROUP_WIDTH // LANES
    vals, lses = [], []
    for gi, (_, dil) in enumerate(ATTN_GROUPS):
        if dil == 1:
            vals.append(o_refs[gi][0].astype(F32))
            lses.append(st_refs[gi][0, 0])
            continue
        rows = tm // dil
        for r in range(dil):
            for s in range(n_slabs):
                lo = r * GROUP_WIDTH + s * LANES
                oscr[gi, s, pl.ds(r, rows, stride=dil), :] = o_refs[gi][0, :, lo:lo + LANES].astype(F32)
            sscr[gi, pl.ds(r, rows, stride=dil), :] = st_refs[gi][0, r]
        vals.append(jnp.concatenate([oscr[gi, s] for s in range(n_slabs)], axis=-1))
        lses.append(sscr[gi])
    top = functools.reduce(jnp.maximum, lses)
    es = [jnp.exp2(l - top) for l in lses]
    inv = 1.0 / functools.reduce(lambda a, c: a + c, es)
    o = None
    for e, val in zip(es, vals):
        alpha = e * inv
        hi = alpha.astype(BF16)
        lo = (alpha - hi.astype(F32)).astype(BF16)
        wide = _dot(hi, ex_ref[...]) + _dot(lo, ex_ref[...])
        o = wide * val if o is None else o + wide * val
    out_ref[0] = x_ref[0] + _dot(o.astype(BF16), wo_ref[...])


def _merge_groups(x, os_, sts, w_o):
    b, s, d = x.shape
    tm = TOKEN_TILE
    ng = len(ATTN_GROUPS)
    src = jnp.arange(LANES)[:, None]
    dst = jnp.arange(GROUP_WIDTH)[None, :]
    expand = (src == (dst // HEAD_DIM) * STAT_LANES).astype(BF16)
    tile = pl.BlockSpec((1, tm, d), lambda i, j: (i, j, 0))
    in_specs = [tile]
    for _, dil in ATTN_GROUPS:
        in_specs.append(pl.BlockSpec((1, tm // dil, dil * GROUP_WIDTH), lambda i, j: (i, j, 0)))
    for _, dil in ATTN_GROUPS:
        in_specs.append(pl.BlockSpec((1, dil, tm // dil, LANES), lambda i, j: (i, 0, j, 0)))
    in_specs += [_const_spec(w_o.shape), _const_spec(expand.shape)]
    return pl.pallas_call(
        functools.partial(_merge_body, tm=tm),
        grid=(b, s // tm),
        in_specs=in_specs,
        out_specs=tile,
        out_shape=jax.ShapeDtypeStruct((b, s, d), F32),
        scratch_shapes=[pltpu.VMEM((ng, GROUP_WIDTH // LANES, tm, LANES), F32),
                        pltpu.VMEM((ng, tm, LANES), F32)],
        compiler_params=_params(2),
        name="attn_merge",
    )(x, *os_, *sts, w_o, expand)


def _attn_mixer(x, g, w_qkv, w_o):
    ng = len(ATTN_GROUPS)
    qkv = _qkv_rope(x, g, w_qkv)
    os_, sts = [], []
    for gi, (_, dil) in enumerate(ATTN_GROUPS):
        o, st = _attn_group(qkv[gi], qkv[ng + gi], qkv[2 * ng + gi], dil)
        os_.append(o)
        sts.append(st)
    return _merge_groups(x, os_, sts, w_o)


def kernel(x, norm_mix_g, norm_ffn_g, final_norm_g, conv_w_in, conv_b_in, conv_w_dw, conv_b_dw,
           conv_ln_g, conv_ln_b, conv_w_out, conv_b_out, attn_w_qkv, attn_w_o, pool_w, pool_scale,
           ffn_w_gate, ffn_w_up, ffn_w_down):
    b, s, d = x.shape
    depth = norm_mix_g.shape[0]
    bf = lambda w: w.astype(BF16)
    for i in range(depth):
        kind, j = i % 3, i // 3
        if kind == 0:
            x = _conv_mixer(x, norm_mix_g[i], bf(conv_w_in[j]), conv_b_in[j], conv_w_dw[j], conv_b_dw[j],
                            conv_ln_g[j], conv_ln_b[j], bf(conv_w_out[j]), conv_b_out[j])
        elif kind == 1:
            x = _attn_mixer(x, norm_mix_g[i], bf(attn_w_qkv[j]), bf(attn_w_o[j]))
        else:
            x = _pool_mixer(x, norm_mix_g[i], bf(pool_w[j]), pool_scale[j])
        last = i == depth - 1
        x = _ffn(x.reshape(b * s, d), norm_ffn_g[i], bf(ffn_w_gate[i]), bf(ffn_w_up[i]), bf(ffn_w_down[i]),
                 final_g=final_norm_g if last else None).reshape(b, s, d)
    return x
```

```python
import functools

import jax
import jax.numpy as jnp
from jax import lax
from jax.experimental import pallas as pl
from jax.experimental.pallas import tpu as pltpu

F32 = jnp.float32
BF16 = jnp.bfloat16

RMS_EPS = 1e-6
LN_EPS = 1e-5

CONV_WIDTH = 31
CONV_HALO = 32
CONV_ROWS = 128

ATTN_GROUPS = ((128, 1), (512, 4), (2048, 16))
HEADS = 8
HEAD_DIM = 64
GROUP_WIDTH = HEADS * HEAD_DIM
ATTN_BLOCK = 128
ATTN_UNROLL = 4
LOG2_E = 1.4426950408889634
Q_SCALE = HEAD_DIM ** -0.5 * LOG2_E
ROPE_THETA = 10000.0
NEG_INF = -1e30

POOL_WINDOWS = (2, 4, 8, 16)
POOL_HALO = 16

LANES = 128
BF16_ROWS = 16
STAT_LANES = LANES // HEADS
VMEM_LIMIT_BYTES = 56 * 1024 * 1024

TOKEN_TILE = 512
QKV_TILE = 1024
FFN_TILE = 1024
FFN_CHUNK = 512


def _const_spec(shape):
    nd = len(shape)
    return pl.BlockSpec(shape, lambda *_: (0,) * nd, pipeline_mode=pl.Buffered(1))


def _layer_spec(shape, layer):
    nd = len(shape)
    return pl.BlockSpec((pl.Squeezed(),) + tuple(shape[1:]), lambda *_: (layer,) + (0,) * (nd - 1),
                        pipeline_mode=pl.Buffered(1))


def _params(n_axes):
    return pltpu.CompilerParams(
        dimension_semantics=("arbitrary",) * n_axes,
        vmem_limit_bytes=VMEM_LIMIT_BYTES,
    )


def _rms(x, g):
    return x * lax.rsqrt(jnp.mean(x * x, axis=-1, keepdims=True) + RMS_EPS) * g


def _dot(a, b):
    return jnp.dot(a, b, preferred_element_type=F32)


def _silu(x):
    return x * jax.nn.sigmoid(x)


def _ffn_body(*refs, chunks, final):
    if final:
        x_ref, g_ref, wg_ref, wu_ref, wd_ref, fg_ref, o_ref = refs
    else:
        x_ref, g_ref, wg_ref, wu_ref, wd_ref, o_ref = refs
    x = x_ref[...]
    h = _rms(x, g_ref[...]).astype(BF16)
    acc = None
    for c0, cw in chunks:
        gate = _dot(h, wg_ref[:, c0:c0 + cw])
        up = _dot(h, wu_ref[:, c0:c0 + cw])
        act = (_silu(gate) * up).astype(BF16)
        part = _dot(act, wd_ref[c0:c0 + cw, :])
        acc = part if acc is None else acc + part
    y = x + acc
    if final:
        y = _rms(y, fg_ref[...])
    o_ref[...] = y


def _hidden_chunks(f):
    chunks, c0 = [], 0
    while c0 < f:
        cw = min(FFN_CHUNK, f - c0)
        chunks.append((c0, cw))
        c0 += cw
    return tuple(chunks)


def _ffn(x2d, g, wg, wu, wd, layer, final_g=None):
    n, d = x2d.shape
    f = wg.shape[2]
    tm = FFN_TILE
    final = final_g is not None
    in_specs = [
        pl.BlockSpec((tm, d), lambda i: (i, 0)),
        _const_spec((1, d)),
        _layer_spec(wg.shape, layer),
        _layer_spec(wu.shape, layer),
        _layer_spec(wd.shape, layer),
    ]
    args = [x2d, g.reshape(1, d), wg, wu, wd]
    if final:
        in_specs.append(_const_spec((1, d)))
        args.append(final_g.reshape(1, d))
    return pl.pallas_call(
        functools.partial(_ffn_body, chunks=_hidden_chunks(f), final=final),
        grid=(n // tm,),
        in_specs=in_specs,
        out_specs=pl.BlockSpec((tm, d), lambda i: (i, 0)),
        out_shape=jax.ShapeDtypeStruct((n, d), F32),
        compiler_params=_params(1),
        name="ffn",
    )(*args)


def _conv_body(x_ref, g_ref, win_ref, bin_ref, wdw_ref, bdw_ref, lng_ref, lnb_ref,
               wout_ref, bout_ref, o_ref, ubuf, even, odd, cbuf, *, tm, d):
    n_slabs = d // LANES
    rows = CONV_ROWS
    span = CONV_HALO + tm

    @pl.when(pl.program_id(1) == 0)
    def _():
        ubuf[:, 0:CONV_HALO, :] = jnp.zeros((n_slabs, CONV_HALO, LANES), F32)
        ubuf[:, span:span + 8, :] = jnp.zeros((n_slabs, 8, LANES), F32)

    x = x_ref[0]
    h = _rms(x, g_ref[...]).astype(BF16)
    a = _dot(h, win_ref[:, 0:d]) + bin_ref[:, 0:d]
    gate = _dot(h, win_ref[:, d:2 * d]) + bin_ref[:, d:2 * d]
    u = a * jax.nn.sigmoid(gate)
    for s in range(n_slabs):
        ubuf[s, CONV_HALO:span, :] = u[:, s * LANES:(s + 1) * LANES]
    for s in range(n_slabs):
        even[s] = pltpu.bitcast(ubuf[s, 0:span, :].astype(BF16), jnp.uint32)
        odd[s] = pltpu.bitcast(ubuf[s, pl.ds(1, span, stride=1), :].astype(BF16), jnp.uint32)

    first = CONV_HALO - (CONV_WIDTH - 1)

    def conv_rows(c, carry):
        base = pl.multiple_of(c * rows, rows)
        word = pl.multiple_of(c * (rows // 2), rows // 2)
        for s in range(n_slabs):
            acc = jnp.zeros((rows, LANES), F32)
            for k in range(CONV_WIDTH):
                off = first + k
                src = even if off % 2 == 0 else odd
                words = src[s, pl.ds(word + off // 2, rows // 2, stride=1), :]
                tap = wdw_ref[s, BF16_ROWS * k:BF16_ROWS * (k + 1), :].astype(F32)
                tap = jnp.concatenate([tap] * (rows // BF16_ROWS), axis=0)
                acc = acc + pltpu.bitcast(words, BF16).astype(F32) * tap
            cbuf[s, pl.ds(base, rows), :] = acc + bdw_ref[s]
        return carry

    lax.fori_loop(0, tm // rows, conv_rows, 0)
    for s in range(n_slabs):
        ubuf[s, 0:CONV_HALO, :] = ubuf[s, tm:span, :]

    cv = jnp.concatenate([cbuf[s] for s in range(n_slabs)], axis=-1)
    mu = jnp.mean(cv, axis=-1, keepdims=True)
    cen = cv - mu
    var = jnp.mean(cen * cen, axis=-1, keepdims=True)
    y = cen * lax.rsqrt(var + LN_EPS) * lng_ref[...] + lnb_ref[...]
    y = _silu(y).astype(BF16)
    o_ref[0] = x + _dot(y, wout_ref[...]) + bout_ref[...]


def _slabs(v):
    rows, d = v.shape
    return v.reshape(rows, d // LANES, LANES).transpose(1, 0, 2)


def _conv_mixer(x, g, w_in, b_in, w_dw, b_dw, ln_g, ln_b, w_out, b_out, layer):
    b, s, d = x.shape
    tm = TOKEN_TILE
    row = lambda v: v.reshape(1, -1)
    tile = pl.BlockSpec((1, tm, d), lambda i, j: (i, j, 0))
    taps = _slabs(jnp.repeat(w_dw.astype(BF16), BF16_ROWS, axis=0))
    n_slabs = d // LANES
    return pl.pallas_call(
        functools.partial(_conv_body, tm=tm, d=d),
        grid=(b, s // tm),
        in_specs=[
            tile,
            _const_spec((1, d)),
            _layer_spec(w_in.shape, layer),
            _const_spec((1, 2 * d)),
            _const_spec(taps.shape),
            _const_spec((n_slabs, 1, LANES)),
            _const_spec((1, d)),
            _const_spec((1, d)),
            _layer_spec(w_out.shape, layer),
            _const_spec((1, d)),
        ],
        out_specs=tile,
        out_shape=jax.ShapeDtypeStruct((b, s, d), F32),
        scratch_shapes=[
            pltpu.VMEM((n_slabs, CONV_HALO + tm + 8, LANES), F32),
            pltpu.VMEM((n_slabs, (CONV_HALO + tm) // 2, LANES), jnp.uint32),
            pltpu.VMEM((n_slabs, (CONV_HALO + tm) // 2, LANES), jnp.uint32),
            pltpu.VMEM((n_slabs, tm, LANES), F32),
        ],
        compiler_params=_params(2),
        name="conv_mixer",
    )(x, row(g), w_in, row(b_in), taps, _slabs(row(b_dw)), row(ln_g), row(ln_b), w_out, row(b_out))


def _pool_body(x_ref, g_ref, w_ref, sc_ref, o_ref, hbuf, *, tm, d):
    j = pl.program_id(1)

    n_slabs = d // LANES

    @pl.when(j == 0)
    def _():
        hbuf[:, 0:POOL_HALO, :] = jnp.zeros((n_slabs, POOL_HALO, LANES), F32)

    x = x_ref[0]
    h = _rms(x, g_ref[...])
    for s in range(n_slabs):
        hbuf[s, POOL_HALO:POOL_HALO + tm, :] = h[:, s * LANES:(s + 1) * LANES]
    pos = j * tm + lax.broadcasted_iota(jnp.int32, (tm, 1), 0) + 1
    per_group = n_slabs // len(POOL_WINDOWS)
    outs = []
    for gi, w in enumerate(POOL_WINDOWS):
        cnt = jnp.minimum(pos, w).astype(F32)
        cols = []
        for s in range(gi * per_group, (gi + 1) * per_group):
            cur = h[:, s * LANES:(s + 1) * LANES]
            tot = cur
            for k in range(1, w):
                tot = tot + hbuf[s, pl.ds(POOL_HALO - k, tm, stride=1), :]
            cols.append(tot / cnt - cur)
        p = jnp.concatenate(cols, axis=-1)
        outs.append(_dot(p.astype(BF16), w_ref[gi]))
    for s in range(n_slabs):
        hbuf[s, 0:POOL_HALO, :] = hbuf[s, tm:tm + POOL_HALO, :]
    o_ref[0] = x + jnp.concatenate(outs, axis=-1) * sc_ref[...]


def _pool_mixer(x, g, w_grp, scale):
    b, s, d = x.shape
    tm = TOKEN_TILE
    tile = pl.BlockSpec((1, tm, d), lambda i, j: (i, j, 0))
    return pl.pallas_call(
        functools.partial(_pool_body, tm=tm, d=d),
        grid=(b, s // tm),
        in_specs=[tile, _const_spec((1, d)), _const_spec(w_grp.shape), _const_spec((1, d))],
        out_specs=tile,
        out_shape=jax.ShapeDtypeStruct((b, s, d), F32),
        scratch_shapes=[pltpu.VMEM((d // LANES, tm + POOL_HALO, LANES), F32)],
        compiler_params=_params(2),
        name="pool_mixer",
    )(x, g.reshape(1, d), w_grp, scale.reshape(1, d))


def _qkv_body(x_ref, g_ref, w_ref, cos_ref, sin_ref, *rest, tm):
    outs, slab = rest[:-1], rest[-1]
    h = _rms(x_ref[0], g_ref[...]).astype(BF16)
    cos = cos_ref[...]
    sin = sin_ref[...]
    lane = lax.broadcasted_iota(jnp.int32, (tm, LANES), 1)
    first_half = (lane % HEAD_DIM) < HEAD_DIM // 2
    n_slabs = GROUP_WIDTH // LANES
    for kind in range(3):
        for gi, (_, dil) in enumerate(ATTN_GROUPS):
            c0 = (kind * len(ATTN_GROUPS) + gi) * GROUP_WIDTH
            y = _dot(h, w_ref[:, c0:c0 + GROUP_WIDTH])
            out = outs[kind * len(ATTN_GROUPS) + gi]
            for s in range(n_slabs):
                ys = y[:, s * LANES:(s + 1) * LANES]
                if kind < 2:
                    half = HEAD_DIM // 2
                    swapped = jnp.where(first_half,
                                        pltpu.roll(ys, LANES - half, axis=1),
                                        pltpu.roll(ys, half, axis=1))
                    ys = ys * cos + swapped * sin
                if kind == 0:
                    ys = ys * Q_SCALE
                if dil == 1:
                    out[0, :, s * LANES:(s + 1) * LANES] = ys.astype(BF16)
                else:
                    slab[s] = ys
            if dil > 1:
                rows = tm // dil
                for r in range(dil):
                    for s in range(n_slabs):
                        lo = r * GROUP_WIDTH + s * LANES
                        out[0, :, lo:lo + LANES] = slab[s, pl.ds(r, rows, stride=dil), :].astype(BF16)


def _rope_tables(s):
    half = HEAD_DIM // 2
    inv_freq = ROPE_THETA ** (-jnp.arange(half, dtype=F32) / half)
    ang = jnp.arange(s, dtype=F32)[:, None] * inv_freq[None, :]
    cos = jnp.tile(jnp.cos(ang), (1, LANES // half))
    sin = jnp.sin(ang)
    sin = jnp.tile(jnp.concatenate([-sin, sin], axis=-1), (1, LANES // HEAD_DIM))
    return cos, sin


def _qkv_rope(x, g, w_qkv):
    b, s, d = x.shape
    tm = QKV_TILE
    cos, sin = _rope_tables(s)
    out_specs, out_shapes = [], []
    for _ in range(3):
        for _, dil in ATTN_GROUPS:
            out_specs.append(pl.BlockSpec((1, tm // dil, dil * GROUP_WIDTH), lambda j, i: (i, j, 0)))
            out_shapes.append(jax.ShapeDtypeStruct((b, s // dil, dil * GROUP_WIDTH), BF16))
    return pl.pallas_call(
        functools.partial(_qkv_body, tm=tm),
        grid=(s // tm, b),
        in_specs=[
            pl.BlockSpec((1, tm, d), lambda j, i: (i, j, 0)),
            _const_spec((1, d)),
            _const_spec(w_qkv.shape),
            pl.BlockSpec((tm, LANES), lambda j, i: (j, 0)),
            pl.BlockSpec((tm, LANES), lambda j, i: (j, 0)),
        ],
        out_specs=out_specs,
        out_shape=out_shapes,
        scratch_shapes=[pltpu.VMEM((GROUP_WIDTH // LANES, tm, LANES), F32)],
        compiler_params=_params(2),
        name="qkv_rope",
    )(x, g.reshape(1, d), w_qkv, cos, sin)


def _attn_body(q_ref, k_ref, v_ref, bias_ref, o_ref, st_ref, *, n, residues):
    blk = ATTN_BLOCK
    lane = lax.broadcasted_iota(jnp.int32, (blk, LANES), 1)
    head0 = lane < HEAD_DIM
    stat_head = lane // STAT_LANES
    ones = jnp.ones((2 * blk, LANES), BF16)

    def one_block(ib, carry):
        row0 = pl.multiple_of(ib * blk, blk)
        prow0 = pl.multiple_of(jnp.maximum(ib - 1, 0) * blk, blk)
        bias = bias_ref[jnp.minimum(ib, 1)]
        for r in range(residues):
            one_residue(r, row0, prow0, bias)
        return carry

    def one_residue(r, row0, prow0, bias):
        stats = jnp.zeros((blk, LANES), F32)
        for p in range(GROUP_WIDTH // LANES):
            lanes = slice(r * GROUP_WIDTH + p * LANES, r * GROUP_WIDTH + (p + 1) * LANES)
            q2 = q_ref[0, pl.ds(row0, blk), lanes]
            k2 = jnp.concatenate([k_ref[0, pl.ds(prow0, blk), lanes],
                                  k_ref[0, pl.ds(row0, blk), lanes]], axis=0)
            v2 = jnp.concatenate([v_ref[0, pl.ds(prow0, blk), lanes],
                                  v_ref[0, pl.ds(row0, blk), lanes]], axis=0)
            zero = jnp.zeros_like(q2)
            qq = jnp.concatenate([jnp.where(head0, q2, zero), jnp.where(head0, zero, q2)], axis=0)
            sc = lax.dot_general(qq, k2, (((1,), (1,)), ((), ())), preferred_element_type=F32) + bias
            m = jnp.max(sc, axis=-1, keepdims=True)
            pe = jnp.exp2(sc - m).astype(BF16)
            pvd = _dot(pe, jnp.concatenate([v2, ones], axis=1))
            den = pvd[:, LANES:]
            pv = pvd[:, :LANES] * (1.0 / den)
            o_ref[0, pl.ds(row0, blk), lanes] = jnp.where(head0, pv[:blk], pv[blk:]).astype(BF16)
            lse = m + jnp.log2(den)
            stats = jnp.where(stat_head == 2 * p, lse[:blk], stats)
            stats = jnp.where(stat_head == 2 * p + 1, lse[blk:], stats)
        st_ref[0, r, pl.ds(row0, blk), :] = stats

    lax.fori_loop(0, n // blk, one_block, 0, unroll=min(ATTN_UNROLL, n // blk))


def _attn_bias():
    blk = ATTN_BLOCK
    qi = jnp.arange(2 * blk)[:, None] % blk
    kj = jnp.arange(2 * blk)[None, :]
    band = (kj >= qi) & (kj <= qi + blk)
    first = band & (kj >= blk)
    return jnp.where(jnp.stack([first, band]), 0.0, NEG_INF).astype(F32)


def _attn_group(q, k, v, dil):
    b, n, _ = q.shape
    residues = min(dil, max(1, ATTN_UNROLL // (n // ATTN_BLOCK)))
    seq = pl.BlockSpec((1, n, residues * GROUP_WIDTH), lambda i, r: (i, 0, r))
    bias = _attn_bias()
    return pl.pallas_call(
        functools.partial(_attn_body, n=n, residues=residues),
        grid=(b, dil // residues),
        in_specs=[seq, seq, seq, _const_spec(bias.shape)],
        out_specs=[seq, pl.BlockSpec((1, residues, n, LANES), lambda i, r: (i, r, 0, 0))],
        out_shape=[jax.ShapeDtypeStruct((b, n, dil * GROUP_WIDTH), BF16),
                   jax.ShapeDtypeStruct((b, dil, n, LANES), F32)],
        compiler_params=_params(2),
        name=f"attn_dil{dil}",
    )(q, k, v, bias)


def _merge_body(x_ref, *rest, tm):
    ng = len(ATTN_GROUPS)
    o_refs, st_refs = rest[:ng], rest[ng:2 * ng]
    wo_ref, ex_ref, out_ref, oscr, sscr = rest[2 * ng:]
    n_slabs = GROUP_WIDTH // LANES
    vals, lses = [], []
    for gi, (_, dil) in enumerate(ATTN_GROUPS):
        if dil == 1:
            vals.append(o_refs[gi][0].astype(F32))
            lses.append(st_refs[gi][0, 0])
            continue
        rows = tm // dil
        for r in range(dil):
            for s in range(n_slabs):
                lo = r * GROUP_WIDTH + s * LANES
                oscr[gi, s, pl.ds(r, rows, stride=dil), :] = o_refs[gi][0, :, lo:lo + LANES].astype(F32)
            sscr[gi, pl.ds(r, rows, stride=dil), :] = st_refs[gi][0, r]
        vals.append(jnp.concatenate([oscr[gi, s] for s in range(n_slabs)], axis=-1))
        lses.append(sscr[gi])
    top = functools.reduce(jnp.maximum, lses)
    es = [jnp.exp2(l - top) for l in lses]
    inv = 1.0 / functools.reduce(lambda a, c: a + c, es)
    o = None
    for e, val in zip(es, vals):
        alpha = e * inv
        hi = alpha.astype(BF16)
        lo = (alpha - hi.astype(F32)).astype(BF16)
        wide = _dot(hi, ex_ref[...]) + _dot(lo, ex_ref[...])
        o = wide * val if o is None else o + wide * val
    out_ref[0] = x_ref[0] + _dot(o.astype(BF16), wo_ref[...])


def _merge_groups(x, os_, sts, w_o):
    b, s, d = x.shape
    tm = TOKEN_TILE
    ng = len(ATTN_GROUPS)
    src = jnp.arange(LANES)[:, None]
    dst = jnp.arange(GROUP_WIDTH)[None, :]
    expand = (src == (dst // HEAD_DIM) * STAT_LANES).astype(BF16)
    tile = pl.BlockSpec((1, tm, d), lambda i, j: (i, j, 0))
    in_specs = [tile]
    for _, dil in ATTN_GROUPS:
        in_specs.append(pl.BlockSpec((1, tm // dil, dil * GROUP_WIDTH), lambda i, j: (i, j, 0)))
    for _, dil in ATTN_GROUPS:
        in_specs.append(pl.BlockSpec((1, dil, tm // dil, LANES), lambda i, j: (i, 0, j, 0)))
    in_specs += [_const_spec(w_o.shape), _const_spec(expand.shape)]
    return pl.pallas_call(
        functools.partial(_merge_body, tm=tm),
        grid=(b, s // tm),
        in_specs=in_specs,
        out_specs=tile,
        out_shape=jax.ShapeDtypeStruct((b, s, d), F32),
        scratch_shapes=[pltpu.VMEM((ng, GROUP_WIDTH // LANES, tm, LANES), F32),
                        pltpu.VMEM((ng, tm, LANES), F32)],
        compiler_params=_params(2),
        name="attn_merge",
    )(x, *os_, *sts, w_o, expand)


def _attn_mixer(x, g, w_qkv, w_o):
    ng = len(ATTN_GROUPS)
    qkv = _qkv_rope(x, g, w_qkv)
    os_, sts = [], []
    for gi, (_, dil) in enumerate(ATTN_GROUPS):
        o, st = _attn_group(qkv[gi], qkv[ng + gi], qkv[2 * ng + gi], dil)
        os_.append(o)
        sts.append(st)
    return _merge_groups(x, os_, sts, w_o)


def kernel(x, norm_mix_g, norm_ffn_g, final_norm_g, conv_w_in, conv_b_in, conv_w_dw, conv_b_dw,
           conv_ln_g, conv_ln_b, conv_w_out, conv_b_out, attn_w_qkv, attn_w_o, pool_w, pool_scale,
           ffn_w_gate, ffn_w_up, ffn_w_down):
    b, s, d = x.shape
    depth = norm_mix_g.shape[0]
    bf = lambda w: w.astype(BF16)
    conv_w_in, conv_w_out = bf(conv_w_in), bf(conv_w_out)
    ffn_w_gate, ffn_w_up, ffn_w_down = bf(ffn_w_gate), bf(ffn_w_up), bf(ffn_w_down)
    for i in range(depth):
        kind, j = i % 3, i // 3
        if kind == 0:
            x = _conv_mixer(x, norm_mix_g[i], conv_w_in, conv_b_in[j], conv_w_dw[j], conv_b_dw[j],
                            conv_ln_g[j], conv_ln_b[j], conv_w_out, conv_b_out[j], layer=j)
        elif kind == 1:
            x = _attn_mixer(x, norm_mix_g[i], bf(attn_w_qkv[j]), bf(attn_w_o[j]))
        else:
            x = _pool_mixer(x, norm_mix_g[i], bf(pool_w[j]), pool_scale[j])
        last = i == depth - 1
        x = _ffn(x.reshape(b * s, d), norm_ffn_g[i], ffn_w_gate, ffn_w_up, ffn_w_down, layer=i,
                 final_g=final_norm_g if last else None).reshape(b, s, d)
    return x
```

```python
import functools

import jax
import jax.numpy as jnp
from jax import lax
from jax.experimental import pallas as pl
from jax.experimental.pallas import tpu as pltpu

F32 = jnp.float32
BF16 = jnp.bfloat16

RMS_EPS = 1e-6
LN_EPS = 1e-5

CONV_WIDTH = 31
CONV_HALO = 32
CONV_ROWS = 128

ATTN_GROUPS = ((128, 1), (512, 4), (2048, 16))
HEADS = 8
HEAD_DIM = 64
GROUP_WIDTH = HEADS * HEAD_DIM
ATTN_BLOCK = 128
ATTN_UNROLL = 4
LOG2_E = 1.4426950408889634
Q_SCALE = HEAD_DIM ** -0.5 * LOG2_E
ROPE_THETA = 10000.0
NEG_INF = -1e30

POOL_WINDOWS = (2, 4, 8, 16)
POOL_HALO = 16

LANES = 128
BF16_ROWS = 16
STAT_LANES = LANES // HEADS
VMEM_LIMIT_BYTES = 56 * 1024 * 1024

TOKEN_TILE = 512
QKV_TILE = 1024
MERGE_TILE = 1024
POOL_TILE = 1024
FFN_TILE = 1024
FFN_CHUNK = 512


def _const_spec(shape):
    nd = len(shape)
    return pl.BlockSpec(shape, lambda *_: (0,) * nd, pipeline_mode=pl.Buffered(1))


def _layer_spec(shape, layer):
    nd = len(shape)
    return pl.BlockSpec((pl.Squeezed(),) + tuple(shape[1:]), lambda *_: (layer,) + (0,) * (nd - 1),
                        pipeline_mode=pl.Buffered(1))


def _params(n_axes):
    return pltpu.CompilerParams(
        dimension_semantics=("arbitrary",) * n_axes,
        vmem_limit_bytes=VMEM_LIMIT_BYTES,
    )


def _rms(x, g):
    return x * lax.rsqrt(jnp.mean(x * x, axis=-1, keepdims=True) + RMS_EPS) * g


def _dot(a, b):
    return jnp.dot(a, b, preferred_element_type=F32)


def _silu(x):
    return x * jax.nn.sigmoid(x)


def _ffn_body(*refs, chunks, final, mixed):
    x_ref, g_ref, wg_ref, wu_ref, wd_ref = refs[:5]
    rest = list(refs[5:])
    o_ref = rest.pop()
    x = x_ref[...]
    if mixed:
        heads_ref, wo_ref = rest.pop(0), rest.pop(0)
        x = x + _dot(heads_ref[...], wo_ref[...])
    fg_ref = rest.pop(0) if final else None
    h = _rms(x, g_ref[...]).astype(BF16)
    acc = None
    for c0, cw in chunks:
        gate = _dot(h, wg_ref[:, c0:c0 + cw])
        up = _dot(h, wu_ref[:, c0:c0 + cw])
        act = (_silu(gate) * up).astype(BF16)
        part = _dot(act, wd_ref[c0:c0 + cw, :])
        acc = part if acc is None else acc + part
    y = x + acc
    if final:
        y = _rms(y, fg_ref[...])
    o_ref[...] = y


def _hidden_chunks(f):
    chunks, c0 = [], 0
    while c0 < f:
        cw = min(FFN_CHUNK, f - c0)
        chunks.append((c0, cw))
        c0 += cw
    return tuple(chunks)


def _ffn(x2d, g, wg, wu, wd, layer, final_g=None, heads=None, w_o=None):
    n, d = x2d.shape
    f = wg.shape[2]
    tm = FFN_TILE
    final = final_g is not None
    in_specs = [
        pl.BlockSpec((tm, d), lambda i: (i, 0)),
        _const_spec((1, d)),
        _layer_spec(wg.shape, layer),
        _layer_spec(wu.shape, layer),
        _layer_spec(wd.shape, layer),
    ]
    args = [x2d, g.reshape(1, d), wg, wu, wd]
    mixed = heads is not None
    if mixed:
        in_specs += [pl.BlockSpec((tm, heads.shape[1]), lambda i: (i, 0)), _const_spec(w_o.shape)]
        args += [heads, w_o]
    if final:
        in_specs.append(_const_spec((1, d)))
        args.append(final_g.reshape(1, d))
    return pl.pallas_call(
        functools.partial(_ffn_body, chunks=_hidden_chunks(f), final=final, mixed=mixed),
        grid=(n // tm,),
        in_specs=in_specs,
        out_specs=pl.BlockSpec((tm, d), lambda i: (i, 0)),
        out_shape=jax.ShapeDtypeStruct((n, d), F32),
        compiler_params=_params(1),
        name="ffn",
    )(*args)


def _conv_body(x_ref, g_ref, win_ref, bin_ref, wdw_ref, bdw_ref, lng_ref, lnb_ref,
               wout_ref, bout_ref, o_ref, ubuf, even, odd, cbuf, *, tm, d):
    n_slabs = d // LANES
    rows = CONV_ROWS
    span = CONV_HALO + tm

    @pl.when(pl.program_id(1) == 0)
    def _():
        ubuf[:, 0:CONV_HALO, :] = jnp.zeros((n_slabs, CONV_HALO, LANES), F32)
        ubuf[:, span:span + 8, :] = jnp.zeros((n_slabs, 8, LANES), F32)

    x = x_ref[0]
    h = _rms(x, g_ref[...]).astype(BF16)
    a = _dot(h, win_ref[:, 0:d]) + bin_ref[:, 0:d]
    gate = _dot(h, win_ref[:, d:2 * d]) + bin_ref[:, d:2 * d]
    u = a * jax.nn.sigmoid(gate)
    for s in range(n_slabs):
        ubuf[s, CONV_HALO:span, :] = u[:, s * LANES:(s + 1) * LANES]
    for s in range(n_slabs):
        even[s] = pltpu.bitcast(ubuf[s, 0:span, :].astype(BF16), jnp.uint32)
        odd[s] = pltpu.bitcast(ubuf[s, pl.ds(1, span, stride=1), :].astype(BF16), jnp.uint32)

    first = CONV_HALO - (CONV_WIDTH - 1)

    def conv_rows(c, carry):
        base = pl.multiple_of(c * rows, rows)
        word = pl.multiple_of(c * (rows // 2), rows // 2)
        for s in range(n_slabs):
            acc = jnp.zeros((rows, LANES), F32)
            for k in range(CONV_WIDTH):
                off = first + k
                src = even if off % 2 == 0 else odd
                words = src[s, pl.ds(word + off // 2, rows // 2, stride=1), :]
                tap = wdw_ref[s, BF16_ROWS * k:BF16_ROWS * (k + 1), :].astype(F32)
                tap = jnp.concatenate([tap] * (rows // BF16_ROWS), axis=0)
                acc = acc + pltpu.bitcast(words, BF16).astype(F32) * tap
            cbuf[s, pl.ds(base, rows), :] = acc + bdw_ref[s]
        return carry

    lax.fori_loop(0, tm // rows, conv_rows, 0)
    for s in range(n_slabs):
        ubuf[s, 0:CONV_HALO, :] = ubuf[s, tm:span, :]

    cv = jnp.concatenate([cbuf[s] for s in range(n_slabs)], axis=-1)
    mu = jnp.mean(cv, axis=-1, keepdims=True)
    cen = cv - mu
    var = jnp.mean(cen * cen, axis=-1, keepdims=True)
    y = cen * lax.rsqrt(var + LN_EPS) * lng_ref[...] + lnb_ref[...]
    y = _silu(y).astype(BF16)
    o_ref[0] = x + _dot(y, wout_ref[...]) + bout_ref[...]


def _slabs(v):
    rows, d = v.shape
    return v.reshape(rows, d // LANES, LANES).transpose(1, 0, 2)


def _conv_mixer(x, g, w_in, b_in, w_dw, b_dw, ln_g, ln_b, w_out, b_out, layer):
    b, s, d = x.shape
    tm = TOKEN_TILE
    row = lambda v: v.reshape(1, -1)
    tile = pl.BlockSpec((1, tm, d), lambda i, j: (i, j, 0))
    taps = _slabs(jnp.repeat(w_dw.astype(BF16), BF16_ROWS, axis=0))
    n_slabs = d // LANES
    return pl.pallas_call(
        functools.partial(_conv_body, tm=tm, d=d),
        grid=(b, s // tm),
        in_specs=[
            tile,
            _const_spec((1, d)),
            _layer_spec(w_in.shape, layer),
            _const_spec((1, 2 * d)),
            _const_spec(taps.shape),
            _const_spec((n_slabs, 1, LANES)),
            _const_spec((1, d)),
            _const_spec((1, d)),
            _layer_spec(w_out.shape, layer),
            _const_spec((1, d)),
        ],
        out_specs=tile,
        out_shape=jax.ShapeDtypeStruct((b, s, d), F32),
        scratch_shapes=[
            pltpu.VMEM((n_slabs, CONV_HALO + tm + 8, LANES), F32),
            pltpu.VMEM((n_slabs, (CONV_HALO + tm) // 2, LANES), jnp.uint32),
            pltpu.VMEM((n_slabs, (CONV_HALO + tm) // 2, LANES), jnp.uint32),
            pltpu.VMEM((n_slabs, tm, LANES), F32),
        ],
        compiler_params=_params(2),
        name="conv_mixer",
    )(x, row(g), w_in, row(b_in), taps, _slabs(row(b_dw)), row(ln_g), row(ln_b), w_out, row(b_out))


def _pool_body(x_ref, g_ref, w_ref, sc_ref, o_ref, hbuf, *, tm, d):
    j = pl.program_id(1)

    n_slabs = d // LANES

    @pl.when(j == 0)
    def _():
        hbuf[:, 0:POOL_HALO, :] = jnp.zeros((n_slabs, POOL_HALO, LANES), F32)

    x = x_ref[0]
    h = _rms(x, g_ref[...])
    for s in range(n_slabs):
        hbuf[s, POOL_HALO:POOL_HALO + tm, :] = h[:, s * LANES:(s + 1) * LANES]
    pos = j * tm + lax.broadcasted_iota(jnp.int32, (tm, 1), 0) + 1
    per_group = n_slabs // len(POOL_WINDOWS)
    outs = []
    for gi, w in enumerate(POOL_WINDOWS):
        cnt = jnp.minimum(pos, w).astype(F32)
        cols = []
        for s in range(gi * per_group, (gi + 1) * per_group):
            cur = h[:, s * LANES:(s + 1) * LANES]
            tot = cur
            for k in range(1, w):
                tot = tot + hbuf[s, pl.ds(POOL_HALO - k, tm, stride=1), :]
            cols.append(tot / cnt - cur)
        p = jnp.concatenate(cols, axis=-1)
        outs.append(_dot(p.astype(BF16), w_ref[gi]))
    for s in range(n_slabs):
        hbuf[s, 0:POOL_HALO, :] = hbuf[s, tm:tm + POOL_HALO, :]
    o_ref[0] = x + jnp.concatenate(outs, axis=-1) * sc_ref[...]


def _pool_mixer(x, g, w_grp, scale):
    b, s, d = x.shape
    tm = POOL_TILE
    tile = pl.BlockSpec((1, tm, d), lambda i, j: (i, j, 0))
    return pl.pallas_call(
        functools.partial(_pool_body, tm=tm, d=d),
        grid=(b, s // tm),
        in_specs=[tile, _const_spec((1, d)), _const_spec(w_grp.shape), _const_spec((1, d))],
        out_specs=tile,
        out_shape=jax.ShapeDtypeStruct((b, s, d), F32),
        scratch_shapes=[pltpu.VMEM((d // LANES, tm + POOL_HALO, LANES), F32)],
        compiler_params=_params(2),
        name="pool_mixer",
    )(x, g.reshape(1, d), w_grp, scale.reshape(1, d))


def _qkv_body(x_ref, g_ref, w_ref, cos_ref, sin_ref, *rest, tm):
    outs, slab = rest[:-1], rest[-1]
    h = _rms(x_ref[0], g_ref[...]).astype(BF16)
    cos = cos_ref[...]
    sin = sin_ref[...]
    lane = lax.broadcasted_iota(jnp.int32, (tm, LANES), 1)
    first_half = (lane % HEAD_DIM) < HEAD_DIM // 2
    n_slabs = GROUP_WIDTH // LANES
    for kind in range(3):
        for gi, (_, dil) in enumerate(ATTN_GROUPS):
            c0 = (kind * len(ATTN_GROUPS) + gi) * GROUP_WIDTH
            y = _dot(h, w_ref[:, c0:c0 + GROUP_WIDTH])
            out = outs[kind * len(ATTN_GROUPS) + gi]
            for s in range(n_slabs):
                ys = y[:, s * LANES:(s + 1) * LANES]
                if kind < 2:
                    half = HEAD_DIM // 2
                    swapped = jnp.where(first_half,
                                        pltpu.roll(ys, LANES - half, axis=1),
                                        pltpu.roll(ys, half, axis=1))
                    ys = ys * cos + swapped * sin
                if kind == 0:
                    ys = ys * Q_SCALE
                if dil == 1:
                    out[0, :, s * LANES:(s + 1) * LANES] = ys.astype(BF16)
                else:
                    slab[s] = ys
            if dil > 1:
                rows = tm // dil
                for r in range(dil):
                    for s in range(n_slabs):
                        lo = r * GROUP_WIDTH + s * LANES
                        out[0, :, lo:lo + LANES] = slab[s, pl.ds(r, rows, stride=dil), :].astype(BF16)


def _rope_tables(s):
    half = HEAD_DIM // 2
    inv_freq = ROPE_THETA ** (-jnp.arange(half, dtype=F32) / half)
    ang = jnp.arange(s, dtype=F32)[:, None] * inv_freq[None, :]
    cos = jnp.tile(jnp.cos(ang), (1, LANES // half))
    sin = jnp.sin(ang)
    sin = jnp.tile(jnp.concatenate([-sin, sin], axis=-1), (1, LANES // HEAD_DIM))
    return cos, sin


def _qkv_rope(x, g, w_qkv):
    b, s, d = x.shape
    tm = QKV_TILE
    cos, sin = _rope_tables(s)
    out_specs, out_shapes = [], []
    for _ in range(3):
        for _, dil in ATTN_GROUPS:
            out_specs.append(pl.BlockSpec((1, tm // dil, dil * GROUP_WIDTH), lambda j, i: (i, j, 0)))
            out_shapes.append(jax.ShapeDtypeStruct((b, s // dil, dil * GROUP_WIDTH), BF16))
    return pl.pallas_call(
        functools.partial(_qkv_body, tm=tm),
        grid=(s // tm, b),
        in_specs=[
            pl.BlockSpec((1, tm, d), lambda j, i: (i, j, 0)),
            _const_spec((1, d)),
            _const_spec(w_qkv.shape),
            pl.BlockSpec((tm, LANES), lambda j, i: (j, 0)),
            pl.BlockSpec((tm, LANES), lambda j, i: (j, 0)),
        ],
        out_specs=out_specs,
        out_shape=out_shapes,
        scratch_shapes=[pltpu.VMEM((GROUP_WIDTH // LANES, tm, LANES), F32)],
        compiler_params=_params(2),
        name="qkv_rope",
    )(x, g.reshape(1, d), w_qkv, cos, sin)


def _attn_body(q_ref, k_ref, v_ref, bias_ref, o_ref, st_ref, *, n, residues):
    blk = ATTN_BLOCK
    lane = lax.broadcasted_iota(jnp.int32, (blk, LANES), 1)
    head0 = lane < HEAD_DIM
    stat_head = lane // STAT_LANES
    ones = jnp.ones((2 * blk, LANES), BF16)

    def one_block(ib, carry):
        row0 = pl.multiple_of(ib * blk, blk)
        prow0 = pl.multiple_of(jnp.maximum(ib - 1, 0) * blk, blk)
        bias = bias_ref[jnp.minimum(ib, 1)]
        for r in range(residues):
            one_residue(r, row0, prow0, bias)
        return carry

    def one_residue(r, row0, prow0, bias):
        stats = jnp.zeros((blk, LANES), F32)
        for p in range(GROUP_WIDTH // LANES):
            lanes = slice(r * GROUP_WIDTH + p * LANES, r * GROUP_WIDTH + (p + 1) * LANES)
            q2 = q_ref[0, pl.ds(row0, blk), lanes]
            k2 = jnp.concatenate([k_ref[0, pl.ds(prow0, blk), lanes],
                                  k_ref[0, pl.ds(row0, blk), lanes]], axis=0)
            v2 = jnp.concatenate([v_ref[0, pl.ds(prow0, blk), lanes],
                                  v_ref[0, pl.ds(row0, blk), lanes]], axis=0)
            zero = jnp.zeros_like(q2)
            qq = jnp.concatenate([jnp.where(head0, q2, zero), jnp.where(head0, zero, q2)], axis=0)
            sc = lax.dot_general(qq, k2, (((1,), (1,)), ((), ())), preferred_element_type=F32) + bias
            m = jnp.max(sc, axis=-1, keepdims=True)
            pe = jnp.exp2(sc - m).astype(BF16)
            pvd = _dot(pe, jnp.concatenate([v2, ones], axis=1))
            den = pvd[:, LANES:]
            pv = pvd[:, :LANES] * (1.0 / den)
            o_ref[0, pl.ds(row0, blk), lanes] = jnp.where(head0, pv[:blk], pv[blk:]).astype(BF16)
            lse = m + jnp.log2(den)
            stats = jnp.where(stat_head == 2 * p, lse[:blk], stats)
            stats = jnp.where(stat_head == 2 * p + 1, lse[blk:], stats)
        st_ref[0, r, pl.ds(row0, blk), :] = stats

    lax.fori_loop(0, n // blk, one_block, 0, unroll=min(ATTN_UNROLL, n // blk))


def _attn_bias():
    blk = ATTN_BLOCK
    qi = jnp.arange(2 * blk)[:, None] % blk
    kj = jnp.arange(2 * blk)[None, :]
    band = (kj >= qi) & (kj <= qi + blk)
    first = band & (kj >= blk)
    return jnp.where(jnp.stack([first, band]), 0.0, NEG_INF).astype(F32)


def _attn_group(q, k, v, dil):
    b, n, _ = q.shape
    residues = min(dil, max(1, ATTN_UNROLL // (n // ATTN_BLOCK)))
    seq = pl.BlockSpec((1, n, residues * GROUP_WIDTH), lambda i, r: (i, 0, r))
    bias = _attn_bias()
    return pl.pallas_call(
        functools.partial(_attn_body, n=n, residues=residues),
        grid=(b, dil // residues),
        in_specs=[seq, seq, seq, _const_spec(bias.shape)],
        out_specs=[seq, pl.BlockSpec((1, residues, n, LANES), lambda i, r: (i, r, 0, 0))],
        out_shape=[jax.ShapeDtypeStruct((b, n, dil * GROUP_WIDTH), BF16),
                   jax.ShapeDtypeStruct((b, dil, n, LANES), F32)],
        compiler_params=_params(2),
        name=f"attn_dil{dil}",
    )(q, k, v, bias)


def _merge_body(*refs, tm):
    ng = len(ATTN_GROUPS)
    o_refs, st_refs = refs[:ng], refs[ng:2 * ng]
    ex_ref, out_ref, oscr, sscr = refs[2 * ng:]
    n_slabs = GROUP_WIDTH // LANES
    vals, lses = [], []
    for gi, (_, dil) in enumerate(ATTN_GROUPS):
        if dil == 1:
            vals.append(o_refs[gi][0].astype(F32))
            lses.append(st_refs[gi][0, 0])
            continue
        rows = tm // dil
        for r in range(dil):
            for s in range(n_slabs):
                lo = r * GROUP_WIDTH + s * LANES
                oscr[gi, s, pl.ds(r, rows, stride=dil), :] = o_refs[gi][0, :, lo:lo + LANES].astype(F32)
            sscr[gi, pl.ds(r, rows, stride=dil), :] = st_refs[gi][0, r]
        vals.append(jnp.concatenate([oscr[gi, s] for s in range(n_slabs)], axis=-1))
        lses.append(sscr[gi])
    top = functools.reduce(jnp.maximum, lses)
    es = [jnp.exp2(l - top) for l in lses]
    inv = 1.0 / functools.reduce(lambda a, c: a + c, es)
    o = None
    for e, val in zip(es, vals):
        alpha = e * inv
        hi = alpha.astype(BF16)
        lo = (alpha - hi.astype(F32)).astype(BF16)
        wide = _dot(hi, ex_ref[...]) + _dot(lo, ex_ref[...])
        o = wide * val if o is None else o + wide * val
    out_ref[0] = o.astype(BF16)


def _merge_groups(os_, sts):
    b = os_[0].shape[0]
    s = os_[0].shape[1] * os_[0].shape[2] // GROUP_WIDTH
    tm = MERGE_TILE
    ng = len(ATTN_GROUPS)
    src = jnp.arange(LANES)[:, None]
    dst = jnp.arange(GROUP_WIDTH)[None, :]
    expand = (src == (dst // HEAD_DIM) * STAT_LANES).astype(BF16)
    in_specs = []
    for _, dil in ATTN_GROUPS:
        in_specs.append(pl.BlockSpec((1, tm // dil, dil * GROUP_WIDTH), lambda i, j: (i, j, 0)))
    for _, dil in ATTN_GROUPS:
        in_specs.append(pl.BlockSpec((1, dil, tm // dil, LANES), lambda i, j: (i, 0, j, 0)))
    in_specs.append(_const_spec(expand.shape))
    return pl.pallas_call(
        functools.partial(_merge_body, tm=tm),
        grid=(b, s // tm),
        in_specs=in_specs,
        out_specs=pl.BlockSpec((1, tm, GROUP_WIDTH), lambda i, j: (i, j, 0)),
        out_shape=jax.ShapeDtypeStruct((b, s, GROUP_WIDTH), BF16),
        scratch_shapes=[pltpu.VMEM((ng, GROUP_WIDTH // LANES, tm, LANES), F32),
                        pltpu.VMEM((ng, tm, LANES), F32)],
        compiler_params=_params(2),
        name="attn_merge",
    )(*os_, *sts, expand)


def _attn_heads(x, g, w_qkv):
    ng = len(ATTN_GROUPS)
    qkv = _qkv_rope(x, g, w_qkv)
    os_, sts = [], []
    for gi, (_, dil) in enumerate(ATTN_GROUPS):
        o, st = _attn_group(qkv[gi], qkv[ng + gi], qkv[2 * ng + gi], dil)
        os_.append(o)
        sts.append(st)
    return _merge_groups(os_, sts)


def kernel(x, norm_mix_g, norm_ffn_g, final_norm_g, conv_w_in, conv_b_in, conv_w_dw, conv_b_dw,
           conv_ln_g, conv_ln_b, conv_w_out, conv_b_out, attn_w_qkv, attn_w_o, pool_w, pool_scale,
           ffn_w_gate, ffn_w_up, ffn_w_down):
    b, s, d = x.shape
    depth = norm_mix_g.shape[0]
    bf = lambda w: w.astype(BF16)
    conv_w_in, conv_w_out = bf(conv_w_in), bf(conv_w_out)
    ffn_w_gate, ffn_w_up, ffn_w_down = bf(ffn_w_gate), bf(ffn_w_up), bf(ffn_w_down)
    for i in range(depth):
        kind, j = i % 3, i // 3
        heads = w_o = None
        if kind == 0:
            x = _conv_mixer(x, norm_mix_g[i], conv_w_in, conv_b_in[j], conv_w_dw[j], conv_b_dw[j],
                            conv_ln_g[j], conv_ln_b[j], conv_w_out, conv_b_out[j], layer=j)
        elif kind == 1:
            heads = _attn_heads(x, norm_mix_g[i], bf(attn_w_qkv[j])).reshape(b * s, GROUP_WIDTH)
            w_o = bf(attn_w_o[j])
        else:
            x = _pool_mixer(x, norm_mix_g[i], bf(pool_w[j]), pool_scale[j])
        last = i == depth - 1
        x = _ffn(x.reshape(b * s, d), norm_ffn_g[i], ffn_w_gate, ffn_w_up, ffn_w_down, layer=i,
                 final_g=final_norm_g if last else None, heads=heads, w_o=w_o).reshape(b, s, d)
    return x
```

```python
import functools

import jax
import jax.numpy as jnp
from jax import lax
from jax.experimental import pallas as pl
from jax.experimental.pallas import tpu as pltpu

F32 = jnp.float32
BF16 = jnp.bfloat16

RMS_EPS = 1e-6
LN_EPS = 1e-5

CONV_WIDTH = 31
CONV_HALO = 32
CONV_ROWS = 128

ATTN_GROUPS = ((128, 1), (512, 4), (2048, 16))
HEADS = 8
HEAD_DIM = 64
GROUP_WIDTH = HEADS * HEAD_DIM
ATTN_BLOCK = 128
ATTN_UNROLL = 4
LOG2_E = 1.4426950408889634
Q_SCALE = HEAD_DIM ** -0.5 * LOG2_E
ROPE_THETA = 10000.0
NEG_INF = -1e30

POOL_WINDOWS = (2, 4, 8, 16)
POOL_HALO = 16

LANES = 128
BF16_ROWS = 16
STAT_LANES = LANES // HEADS
VMEM_LIMIT_BYTES = 56 * 1024 * 1024

CONV_TILE = 1024
QKV_TILE = 1024
MERGE_TILE = 1024
POOL_TILE = 1024
FFN_TILE = 1024
FFN_CHUNK = 512


def _const_spec(shape):
    nd = len(shape)
    return pl.BlockSpec(shape, lambda *_: (0,) * nd, pipeline_mode=pl.Buffered(1))


def _layer_spec(shape, layer):
    nd = len(shape)
    return pl.BlockSpec((pl.Squeezed(),) + tuple(shape[1:]), lambda *_: (layer,) + (0,) * (nd - 1),
                        pipeline_mode=pl.Buffered(1))


def _params(n_axes):
    return pltpu.CompilerParams(
        dimension_semantics=("arbitrary",) * n_axes,
        vmem_limit_bytes=VMEM_LIMIT_BYTES,
    )


def _rms(x, g):
    return x * lax.rsqrt(jnp.mean(x * x, axis=-1, keepdims=True) + RMS_EPS) * g


def _dot(a, b):
    return jnp.dot(a, b, preferred_element_type=F32)


def _silu(x):
    return x * jax.nn.sigmoid(x)


def _ffn_body(*refs, chunks, final, mixed):
    x_ref, g_ref, wg_ref, wu_ref, wd_ref = refs[:5]
    rest = list(refs[5:])
    o_ref = rest.pop()
    x = x_ref[...]
    if mixed:
        heads_ref, wo_ref = rest.pop(0), rest.pop(0)
        x = x + _dot(heads_ref[...], wo_ref[...])
    fg_ref = rest.pop(0) if final else None
    h = _rms(x, g_ref[...]).astype(BF16)
    acc = None
    for c0, cw in chunks:
        gate = _dot(h, wg_ref[:, c0:c0 + cw])
        up = _dot(h, wu_ref[:, c0:c0 + cw])
        act = (_silu(gate) * up).astype(BF16)
        part = _dot(act, wd_ref[c0:c0 + cw, :])
        acc = part if acc is None else acc + part
    y = x + acc
    if final:
        y = _rms(y, fg_ref[...])
    o_ref[...] = y


def _hidden_chunks(f):
    chunks, c0 = [], 0
    while c0 < f:
        cw = min(FFN_CHUNK, f - c0)
        chunks.append((c0, cw))
        c0 += cw
    return tuple(chunks)


def _ffn(x2d, g, wg, wu, wd, layer, final_g=None, heads=None, w_o=None):
    n, d = x2d.shape
    f = wg.shape[2]
    tm = FFN_TILE
    final = final_g is not None
    in_specs = [
        pl.BlockSpec((tm, d), lambda i: (i, 0)),
        _const_spec((1, d)),
        _layer_spec(wg.shape, layer),
        _layer_spec(wu.shape, layer),
        _layer_spec(wd.shape, layer),
    ]
    args = [x2d, g.reshape(1, d), wg, wu, wd]
    mixed = heads is not None
    if mixed:
        in_specs += [pl.BlockSpec((tm, heads.shape[1]), lambda i: (i, 0)), _const_spec(w_o.shape)]
        args += [heads, w_o]
    if final:
        in_specs.append(_const_spec((1, d)))
        args.append(final_g.reshape(1, d))
    return pl.pallas_call(
        functools.partial(_ffn_body, chunks=_hidden_chunks(f), final=final, mixed=mixed),
        grid=(n // tm,),
        in_specs=in_specs,
        out_specs=pl.BlockSpec((tm, d), lambda i: (i, 0)),
        out_shape=jax.ShapeDtypeStruct((n, d), F32),
        compiler_params=_params(1),
        name="ffn",
    )(*args)


def _conv_body(x_ref, g_ref, win_ref, bin_ref, wdw_ref, bdw_ref, lng_ref, lnb_ref,
               wout_ref, bout_ref, o_ref, ubuf, even, odd, cbuf, *, tm, d):
    n_slabs = d // LANES
    rows = CONV_ROWS
    span = CONV_HALO + tm

    @pl.when(pl.program_id(1) == 0)
    def _():
        ubuf[:, 0:CONV_HALO, :] = jnp.zeros((n_slabs, CONV_HALO, LANES), F32)
        ubuf[:, span:span + 8, :] = jnp.zeros((n_slabs, 8, LANES), F32)

    x = x_ref[0]
    h = _rms(x, g_ref[...]).astype(BF16)
    a = _dot(h, win_ref[:, 0:d]) + bin_ref[:, 0:d]
    gate = _dot(h, win_ref[:, d:2 * d]) + bin_ref[:, d:2 * d]
    u = a * jax.nn.sigmoid(gate)
    for s in range(n_slabs):
        ubuf[s, CONV_HALO:span, :] = u[:, s * LANES:(s + 1) * LANES]
    for s in range(n_slabs):
        even[s] = pltpu.bitcast(ubuf[s, 0:span, :].astype(BF16), jnp.uint32)
        odd[s] = pltpu.bitcast(ubuf[s, pl.ds(1, span, stride=1), :].astype(BF16), jnp.uint32)

    first = CONV_HALO - (CONV_WIDTH - 1)

    def conv_rows(c, carry):
        base = pl.multiple_of(c * rows, rows)
        word = pl.multiple_of(c * (rows // 2), rows // 2)
        for s in range(n_slabs):
            acc = jnp.zeros((rows, LANES), F32)
            for k in range(CONV_WIDTH):
                off = first + k
                src = even if off % 2 == 0 else odd
                words = src[s, pl.ds(word + off // 2, rows // 2, stride=1), :]
                tap = wdw_ref[s, BF16_ROWS * k:BF16_ROWS * (k + 1), :].astype(F32)
                tap = jnp.concatenate([tap] * (rows // BF16_ROWS), axis=0)
                acc = acc + pltpu.bitcast(words, BF16).astype(F32) * tap
            cbuf[s, pl.ds(base, rows), :] = acc + bdw_ref[s]
        return carry

    lax.fori_loop(0, tm // rows, conv_rows, 0)
    for s in range(n_slabs):
        ubuf[s, 0:CONV_HALO, :] = ubuf[s, tm:span, :]

    cv = jnp.concatenate([cbuf[s] for s in range(n_slabs)], axis=-1)
    mu = jnp.mean(cv, axis=-1, keepdims=True)
    cen = cv - mu
    var = jnp.mean(cen * cen, axis=-1, keepdims=True)
    y = cen * lax.rsqrt(var + LN_EPS) * lng_ref[...] + lnb_ref[...]
    y = _silu(y).astype(BF16)
    o_ref[0] = x + _dot(y, wout_ref[...]) + bout_ref[...]


def _slabs(v):
    rows, d = v.shape
    return v.reshape(rows, d // LANES, LANES).transpose(1, 0, 2)


def _conv_mixer(x, g, w_in, b_in, w_dw, b_dw, ln_g, ln_b, w_out, b_out, layer):
    b, s, d = x.shape
    tm = CONV_TILE
    row = lambda v: v.reshape(1, -1)
    tile = pl.BlockSpec((1, tm, d), lambda i, j: (i, j, 0))
    taps = _slabs(jnp.repeat(w_dw.astype(BF16), BF16_ROWS, axis=0))
    n_slabs = d // LANES
    return pl.pallas_call(
        functools.partial(_conv_body, tm=tm, d=d),
        grid=(b, s // tm),
        in_specs=[
            tile,
            _const_spec((1, d)),
            _layer_spec(w_in.shape, layer),
            _const_spec((1, 2 * d)),
            _const_spec(taps.shape),
            _const_spec((n_slabs, 1, LANES)),
            _const_spec((1, d)),
            _const_spec((1, d)),
            _layer_spec(w_out.shape, layer),
            _const_spec((1, d)),
        ],
        out_specs=tile,
        out_shape=jax.ShapeDtypeStruct((b, s, d), F32),
        scratch_shapes=[
            pltpu.VMEM((n_slabs, CONV_HALO + tm + 8, LANES), F32),
            pltpu.VMEM((n_slabs, (CONV_HALO + tm) // 2, LANES), jnp.uint32),
            pltpu.VMEM((n_slabs, (CONV_HALO + tm) // 2, LANES), jnp.uint32),
            pltpu.VMEM((n_slabs, tm, LANES), F32),
        ],
        compiler_params=_params(2),
        name="conv_mixer",
    )(x, row(g), w_in, row(b_in), taps, _slabs(row(b_dw)), row(ln_g), row(ln_b), w_out, row(b_out))


def _pool_body(x_ref, g_ref, w_ref, sc_ref, o_ref, hbuf, *, tm, d):
    j = pl.program_id(1)

    n_slabs = d // LANES

    @pl.when(j == 0)
    def _():
        hbuf[:, 0:POOL_HALO, :] = jnp.zeros((n_slabs, POOL_HALO, LANES), F32)

    x = x_ref[0]
    h = _rms(x, g_ref[...])
    for s in range(n_slabs):
        hbuf[s, POOL_HALO:POOL_HALO + tm, :] = h[:, s * LANES:(s + 1) * LANES]
    pos = j * tm + lax.broadcasted_iota(jnp.int32, (tm, 1), 0) + 1
    per_group = n_slabs // len(POOL_WINDOWS)
    outs = []
    for gi, w in enumerate(POOL_WINDOWS):
        cnt = jnp.minimum(pos, w).astype(F32)
        cols = []
        for s in range(gi * per_group, (gi + 1) * per_group):
            cur = h[:, s * LANES:(s + 1) * LANES]
            tot = cur
            for k in range(1, w):
                tot = tot + hbuf[s, pl.ds(POOL_HALO - k, tm, stride=1), :]
            cols.append(tot / cnt - cur)
        p = jnp.concatenate(cols, axis=-1)
        outs.append(_dot(p.astype(BF16), w_ref[gi]))
    for s in range(n_slabs):
        hbuf[s, 0:POOL_HALO, :] = hbuf[s, tm:tm + POOL_HALO, :]
    o_ref[0] = x + jnp.concatenate(outs, axis=-1) * sc_ref[...]


def _pool_mixer(x, g, w_grp, scale):
    b, s, d = x.shape
    tm = POOL_TILE
    tile = pl.BlockSpec((1, tm, d), lambda i, j: (i, j, 0))
    return pl.pallas_call(
        functools.partial(_pool_body, tm=tm, d=d),
        grid=(b, s // tm),
        in_specs=[tile, _const_spec((1, d)), _const_spec(w_grp.shape), _const_spec((1, d))],
        out_specs=tile,
        out_shape=jax.ShapeDtypeStruct((b, s, d), F32),
        scratch_shapes=[pltpu.VMEM((d // LANES, tm + POOL_HALO, LANES), F32)],
        compiler_params=_params(2),
        name="pool_mixer",
    )(x, g.reshape(1, d), w_grp, scale.reshape(1, d))


def _qkv_body(x_ref, g_ref, w_ref, cos_ref, sin_ref, *rest, tm):
    outs, slab = rest[:-1], rest[-1]
    h = _rms(x_ref[0], g_ref[...]).astype(BF16)
    cos = cos_ref[...]
    sin = sin_ref[...]
    lane = lax.broadcasted_iota(jnp.int32, (tm, LANES), 1)
    first_half = (lane % HEAD_DIM) < HEAD_DIM // 2
    n_slabs = GROUP_WIDTH // LANES
    for kind in range(3):
        for gi, (_, dil) in enumerate(ATTN_GROUPS):
            c0 = (kind * len(ATTN_GROUPS) + gi) * GROUP_WIDTH
            y = _dot(h, w_ref[:, c0:c0 + GROUP_WIDTH])
            out = outs[gi]
            kind_lo = kind * dil * GROUP_WIDTH
            for s in range(n_slabs):
                ys = y[:, s * LANES:(s + 1) * LANES]
                if kind < 2:
                    half = HEAD_DIM // 2
                    swapped = jnp.where(first_half,
                                        pltpu.roll(ys, LANES - half, axis=1),
                                        pltpu.roll(ys, half, axis=1))
                    ys = ys * cos + swapped * sin
                if kind == 0:
                    ys = ys * Q_SCALE
                if dil == 1:
                    out[0, :, kind_lo + s * LANES:kind_lo + (s + 1) * LANES] = ys.astype(BF16)
                else:
                    slab[s] = ys
            if dil > 1:
                rows = tm // dil
                for r in range(dil):
                    for s in range(n_slabs):
                        lo = kind_lo + r * GROUP_WIDTH + s * LANES
                        out[0, :, lo:lo + LANES] = slab[s, pl.ds(r, rows, stride=dil), :].astype(BF16)


def _rope_tables(s):
    half = HEAD_DIM // 2
    inv_freq = ROPE_THETA ** (-jnp.arange(half, dtype=F32) / half)
    ang = jnp.arange(s, dtype=F32)[:, None] * inv_freq[None, :]
    cos = jnp.tile(jnp.cos(ang), (1, LANES // half))
    sin = jnp.sin(ang)
    sin = jnp.tile(jnp.concatenate([-sin, sin], axis=-1), (1, LANES // HEAD_DIM))
    return cos, sin


def _qkv_rope(x, g, w_qkv):
    b, s, d = x.shape
    tm = QKV_TILE
    cos, sin = _rope_tables(s)
    out_specs, out_shapes = [], []
    for _, dil in ATTN_GROUPS:
        out_specs.append(pl.BlockSpec((1, tm // dil, 3 * dil * GROUP_WIDTH), lambda j, i: (i, j, 0)))
        out_shapes.append(jax.ShapeDtypeStruct((b, s // dil, 3 * dil * GROUP_WIDTH), BF16))
    return pl.pallas_call(
        functools.partial(_qkv_body, tm=tm),
        grid=(s // tm, b),
        in_specs=[
            pl.BlockSpec((1, tm, d), lambda j, i: (i, j, 0)),
            _const_spec((1, d)),
            _const_spec(w_qkv.shape),
            pl.BlockSpec((tm, LANES), lambda j, i: (j, 0)),
            pl.BlockSpec((tm, LANES), lambda j, i: (j, 0)),
        ],
        out_specs=out_specs,
        out_shape=out_shapes,
        scratch_shapes=[pltpu.VMEM((GROUP_WIDTH // LANES, tm, LANES), F32)],
        compiler_params=_params(2),
        name="qkv_rope",
    )(x, g.reshape(1, d), w_qkv, cos, sin)


def _attn_body(q_ref, k_ref, v_ref, bias_ref, o_ref, st_ref, *, n, residues):
    blk = ATTN_BLOCK
    lane = lax.broadcasted_iota(jnp.int32, (blk, LANES), 1)
    head0 = lane < HEAD_DIM
    stat_head = lane // STAT_LANES
    ones = jnp.ones((2 * blk, LANES), BF16)

    def one_block(ib, carry):
        row0 = pl.multiple_of(ib * blk, blk)
        prow0 = pl.multiple_of(jnp.maximum(ib - 1, 0) * blk, blk)
        bias = bias_ref[jnp.minimum(ib, 1)]
        for r in range(residues):
            one_residue(r, row0, prow0, bias)
        return carry

    def one_residue(r, row0, prow0, bias):
        stats = jnp.zeros((blk, LANES), F32)
        for p in range(GROUP_WIDTH // LANES):
            lanes = slice(r * GROUP_WIDTH + p * LANES, r * GROUP_WIDTH + (p + 1) * LANES)
            q2 = q_ref[0, pl.ds(row0, blk), lanes]
            k2 = jnp.concatenate([k_ref[0, pl.ds(prow0, blk), lanes],
                                  k_ref[0, pl.ds(row0, blk), lanes]], axis=0)
            v2 = jnp.concatenate([v_ref[0, pl.ds(prow0, blk), lanes],
                                  v_ref[0, pl.ds(row0, blk), lanes]], axis=0)
            zero = jnp.zeros_like(q2)
            qq = jnp.concatenate([jnp.where(head0, q2, zero), jnp.where(head0, zero, q2)], axis=0)
            sc = lax.dot_general(qq, k2, (((1,), (1,)), ((), ())), preferred_element_type=F32) + bias
            m = jnp.max(sc, axis=-1, keepdims=True)
            pe = jnp.exp2(sc - m).astype(BF16)
            pvd = _dot(pe, jnp.concatenate([v2, ones], axis=1))
            den = pvd[:, LANES:]
            pv = pvd[:, :LANES] * (1.0 / den)
            o_ref[0, pl.ds(row0, blk), lanes] = jnp.where(head0, pv[:blk], pv[blk:]).astype(BF16)
            lse = m + jnp.log2(den)
            stats = jnp.where(stat_head == 2 * p, lse[:blk], stats)
            stats = jnp.where(stat_head == 2 * p + 1, lse[blk:], stats)
        st_ref[0, r, pl.ds(row0, blk), :] = stats

    lax.fori_loop(0, n // blk, one_block, 0, unroll=min(max(1, ATTN_UNROLL // residues), n // blk))


def _attn_bias():
    blk = ATTN_BLOCK
    qi = jnp.arange(2 * blk)[:, None] % blk
    kj = jnp.arange(2 * blk)[None, :]
    band = (kj >= qi) & (kj <= qi + blk)
    first = band & (kj >= blk)
    return jnp.where(jnp.stack([first, band]), 0.0, NEG_INF).astype(F32)


def _attn_group(qkv, dil):
    b, n, _ = qkv.shape
    residues = min(dil, ATTN_UNROLL)
    steps = dil // residues
    width = residues * GROUP_WIDTH
    seq = pl.BlockSpec((1, n, width), lambda i, r: (i, 0, r))
    kinds = [pl.BlockSpec((1, n, width), functools.partial(lambda i, r, kind: (i, 0, kind * steps + r), kind=kind))
             for kind in range(3)]
    bias = _attn_bias()
    return pl.pallas_call(
        functools.partial(_attn_body, n=n, residues=residues),
        grid=(b, steps),
        in_specs=kinds + [_const_spec(bias.shape)],
        out_specs=[seq, pl.BlockSpec((1, residues, n, LANES), lambda i, r: (i, r, 0, 0))],
        out_shape=[jax.ShapeDtypeStruct((b, n, dil * GROUP_WIDTH), BF16),
                   jax.ShapeDtypeStruct((b, dil, n, LANES), F32)],
        compiler_params=_params(2),
        name=f"attn_dil{dil}",
    )(qkv, qkv, qkv, bias)


def _merge_body(*refs, tm):
    ng = len(ATTN_GROUPS)
    o_refs, st_refs = refs[:ng], refs[ng:2 * ng]
    ex_ref, out_ref, oscr, sscr = refs[2 * ng:]
    n_slabs = GROUP_WIDTH // LANES
    vals, lses = [], []
    for gi, (_, dil) in enumerate(ATTN_GROUPS):
        if dil == 1:
            vals.append(o_refs[gi][0].astype(F32))
            lses.append(st_refs[gi][0, 0])
            continue
        rows = tm // dil
        for r in range(dil):
            for s in range(n_slabs):
                lo = r * GROUP_WIDTH + s * LANES
                oscr[gi, s, pl.ds(r, rows, stride=dil), :] = o_refs[gi][0, :, lo:lo + LANES].astype(F32)
            sscr[gi, pl.ds(r, rows, stride=dil), :] = st_refs[gi][0, r]
        vals.append(jnp.concatenate([oscr[gi, s] for s in range(n_slabs)], axis=-1))
        lses.append(sscr[gi])
    top = functools.reduce(jnp.maximum, lses)
    es = [jnp.exp2(l - top) for l in lses]
    inv = 1.0 / functools.reduce(lambda a, c: a + c, es)
    o = None
    for e, val in zip(es, vals):
        alpha = e * inv
        hi = alpha.astype(BF16)
        lo = (alpha - hi.astype(F32)).astype(BF16)
        wide = _dot(hi, ex_ref[...]) + _dot(lo, ex_ref[...])
        o = wide * val if o is None else o + wide * val
    out_ref[0] = o.astype(BF16)


def _merge_groups(os_, sts):
    b = os_[0].shape[0]
    s = os_[0].shape[1] * os_[0].shape[2] // GROUP_WIDTH
    tm = MERGE_TILE
    ng = len(ATTN_GROUPS)
    src = jnp.arange(LANES)[:, None]
    dst = jnp.arange(GROUP_WIDTH)[None, :]
    expand = (src == (dst // HEAD_DIM) * STAT_LANES).astype(BF16)
    in_specs = []
    for _, dil in ATTN_GROUPS:
        in_specs.append(pl.BlockSpec((1, tm // dil, dil * GROUP_WIDTH), lambda i, j: (i, j, 0)))
    for _, dil in ATTN_GROUPS:
        in_specs.append(pl.BlockSpec((1, dil, tm // dil, LANES), lambda i, j: (i, 0, j, 0)))
    in_specs.append(_const_spec(expand.shape))
    return pl.pallas_call(
        functools.partial(_merge_body, tm=tm),
        grid=(b, s // tm),
        in_specs=in_specs,
        out_specs=pl.BlockSpec((1, tm, GROUP_WIDTH), lambda i, j: (i, j, 0)),
        out_shape=jax.ShapeDtypeStruct((b, s, GROUP_WIDTH), BF16),
        scratch_shapes=[pltpu.VMEM((ng, GROUP_WIDTH // LANES, tm, LANES), F32),
                        pltpu.VMEM((ng, tm, LANES), F32)],
        compiler_params=_params(2),
        name="attn_merge",
    )(*os_, *sts, expand)


def _attn_heads(x, g, w_qkv):
    ng = len(ATTN_GROUPS)
    qkv = _qkv_rope(x, g, w_qkv)
    os_, sts = [], []
    for gi, (_, dil) in enumerate(ATTN_GROUPS):
        o, st = _attn_group(qkv[gi], dil)
        os_.append(o)
        sts.append(st)
    return _merge_groups(os_, sts)


def kernel(x, norm_mix_g, norm_ffn_g, final_norm_g, conv_w_in, conv_b_in, conv_w_dw, conv_b_dw,
           conv_ln_g, conv_ln_b, conv_w_out, conv_b_out, attn_w_qkv, attn_w_o, pool_w, pool_scale,
           ffn_w_gate, ffn_w_up, ffn_w_down):
    b, s, d = x.shape
    depth = norm_mix_g.shape[0]
    bf = lambda w: w.astype(BF16)
    conv_w_in, conv_w_out = bf(conv_w_in), bf(conv_w_out)
    ffn_w_gate, ffn_w_up, ffn_w_down = bf(ffn_w_gate), bf(ffn_w_up), bf(ffn_w_down)
    for i in range(depth):
        kind, j = i % 3, i // 3
        heads = w_o = None
        if kind == 0:
            x = _conv_mixer(x, norm_mix_g[i], conv_w_in, conv_b_in[j], conv_w_dw[j], conv_b_dw[j],
                            conv_ln_g[j], conv_ln_b[j], conv_w_out, conv_b_out[j], layer=j)
        elif kind == 1:
            heads = _attn_heads(x, norm_mix_g[i], bf(attn_w_qkv[j])).reshape(b * s, GROUP_WIDTH)
            w_o = bf(attn_w_o[j])
        else:
            x = _pool_mixer(x, norm_mix_g[i], bf(pool_w[j]), pool_scale[j])
        last = i == depth - 1
        x = _ffn(x.reshape(b * s, d), norm_ffn_g[i], ffn_w_gate, ffn_w_up, ffn_w_down, layer=i,
                 final_g=final_norm_g if last else None, heads=heads, w_o=w_o).reshape(b, s, d)
    return x
```

```python
import functools

import jax
import jax.numpy as jnp
from jax import lax
from jax.experimental import pallas as pl
from jax.experimental.pallas import tpu as pltpu

F32 = jnp.float32
BF16 = jnp.bfloat16

RMS_EPS = 1e-6
LN_EPS = 1e-5

CONV_WIDTH = 31
CONV_HALO = 32
CONV_ROWS = 128

ATTN_GROUPS = ((128, 1), (512, 4), (2048, 16))
HEADS = 8
HEAD_DIM = 64
GROUP_WIDTH = HEADS * HEAD_DIM
ATTN_BLOCK = 128
ATTN_UNROLL = 8
LOG2_E = 1.4426950408889634
Q_SCALE = HEAD_DIM ** -0.5 * LOG2_E
ROPE_THETA = 10000.0
NEG_INF = -1e30

POOL_WINDOWS = (2, 4, 8, 16)
POOL_HALO = 16

LANES = 128
BF16_ROWS = 16
STAT_LANES = LANES // HEADS
VMEM_LIMIT_BYTES = 56 * 1024 * 1024

CONV_TILE = 1024
CONV_PARTS = 4
QKV_TILE = 1024
MERGE_TILE = 1024
POOL_TILE = 1024
FFN_TILE = 1024
FFN_CHUNK = 1024


def _const_spec(shape):
    nd = len(shape)
    return pl.BlockSpec(shape, lambda *_: (0,) * nd, pipeline_mode=pl.Buffered(1))


def _layer_spec(shape, layer):
    nd = len(shape)
    return pl.BlockSpec((pl.Squeezed(),) + tuple(shape[1:]), lambda *_: (layer,) + (0,) * (nd - 1),
                        pipeline_mode=pl.Buffered(1))


def _params(n_axes):
    return pltpu.CompilerParams(
        dimension_semantics=("arbitrary",) * n_axes,
        vmem_limit_bytes=VMEM_LIMIT_BYTES,
    )


def _rms(x, g):
    return x * lax.rsqrt(jnp.mean(x * x, axis=-1, keepdims=True) + RMS_EPS) * g


def _dot(a, b):
    return jnp.dot(a, b, preferred_element_type=F32)


def _silu(x):
    return x * jax.nn.sigmoid(x)


def _ffn_body(*refs, chunks, final, mixed):
    x_ref, g_ref, wg_ref, wu_ref, wd_ref = refs[:5]
    rest = list(refs[5:])
    o_ref = rest.pop()
    x = x_ref[...]
    if mixed:
        heads_ref, wo_ref = rest.pop(0), rest.pop(0)
        x = x + _dot(heads_ref[...], wo_ref[...])
    fg_ref = rest.pop(0) if final else None
    h = _rms(x, g_ref[...]).astype(BF16)
    acc = None
    for c0, cw in chunks:
        gate = _dot(h, wg_ref[:, c0:c0 + cw])
        up = _dot(h, wu_ref[:, c0:c0 + cw])
        act = (_silu(gate) * up).astype(BF16)
        part = _dot(act, wd_ref[c0:c0 + cw, :])
        acc = part if acc is None else acc + part
    y = x + acc
    if final:
        y = _rms(y, fg_ref[...])
    o_ref[...] = y


def _hidden_chunks(f):
    chunks, c0 = [], 0
    while c0 < f:
        cw = min(FFN_CHUNK, f - c0)
        chunks.append((c0, cw))
        c0 += cw
    return tuple(chunks)


def _ffn(x2d, g, wg, wu, wd, layer, final_g=None, heads=None, w_o=None):
    n, d = x2d.shape
    f = wg.shape[2]
    tm = FFN_TILE
    final = final_g is not None
    in_specs = [
        pl.BlockSpec((tm, d), lambda i: (i, 0)),
        _const_spec((1, d)),
        _layer_spec(wg.shape, layer),
        _layer_spec(wu.shape, layer),
        _layer_spec(wd.shape, layer),
    ]
    args = [x2d, g.reshape(1, d), wg, wu, wd]
    mixed = heads is not None
    if mixed:
        in_specs += [pl.BlockSpec((tm, heads.shape[1]), lambda i: (i, 0)), _const_spec(w_o.shape)]
        args += [heads, w_o]
    if final:
        in_specs.append(_const_spec((1, d)))
        args.append(final_g.reshape(1, d))
    return pl.pallas_call(
        functools.partial(_ffn_body, chunks=_hidden_chunks(f), final=final, mixed=mixed),
        grid=(n // tm,),
        in_specs=in_specs,
        out_specs=pl.BlockSpec((tm, d), lambda i: (i, 0)),
        out_shape=jax.ShapeDtypeStruct((n, d), F32),
        compiler_params=_params(1),
        name="ffn",
    )(*args)


def _conv_body(x_ref, g_ref, win_ref, bin_ref, wdw_ref, bdw_ref, lng_ref, lnb_ref,
               wout_ref, bout_ref, o_ref, ubuf, even, odd, cbuf, *, tm, d):
    n_slabs = d // LANES
    rows = CONV_ROWS
    span = CONV_HALO + tm

    @pl.when(pl.program_id(1) == 0)
    def _():
        ubuf[:, 0:CONV_HALO, :] = jnp.zeros((n_slabs, CONV_HALO, LANES), F32)
        ubuf[:, span:span + 8, :] = jnp.zeros((n_slabs, 8, LANES), F32)

    part = tm // CONV_PARTS
    for p in range(CONV_PARTS):
        h = _rms(x_ref[0, p * part:(p + 1) * part, :], g_ref[...]).astype(BF16)
        a = _dot(h, win_ref[:, 0:d]) + bin_ref[:, 0:d]
        gate = _dot(h, win_ref[:, d:2 * d]) + bin_ref[:, d:2 * d]
        u = a * jax.nn.sigmoid(gate)
        for s in range(n_slabs):
            ubuf[s, CONV_HALO + p * part:CONV_HALO + (p + 1) * part, :] = u[:, s * LANES:(s + 1) * LANES]
    for s in range(n_slabs):
        even[s] = pltpu.bitcast(ubuf[s, 0:span, :].astype(BF16), jnp.uint32)
        odd[s] = pltpu.bitcast(ubuf[s, pl.ds(1, span, stride=1), :].astype(BF16), jnp.uint32)

    first = CONV_HALO - (CONV_WIDTH - 1)

    def conv_rows(c, carry):
        base = pl.multiple_of(c * rows, rows)
        word = pl.multiple_of(c * (rows // 2), rows // 2)
        for s in range(n_slabs):
            acc = jnp.zeros((rows, LANES), F32)
            for k in range(CONV_WIDTH):
                off = first + k
                src = even if off % 2 == 0 else odd
                words = src[s, pl.ds(word + off // 2, rows // 2, stride=1), :]
                tap = wdw_ref[s, BF16_ROWS * k:BF16_ROWS * (k + 1), :].astype(F32)
                tap = jnp.concatenate([tap] * (rows // BF16_ROWS), axis=0)
                acc = acc + pltpu.bitcast(words, BF16).astype(F32) * tap
            cbuf[s, pl.ds(base, rows), :] = acc + bdw_ref[s]
        return carry

    lax.fori_loop(0, tm // rows, conv_rows, 0)
    for s in range(n_slabs):
        ubuf[s, 0:CONV_HALO, :] = ubuf[s, tm:span, :]

    for p in range(CONV_PARTS):
        rows_p = slice(p * part, (p + 1) * part)
        cv = jnp.concatenate([cbuf[s, rows_p, :] for s in range(n_slabs)], axis=-1)
        mu = jnp.mean(cv, axis=-1, keepdims=True)
        cen = cv - mu
        var = jnp.mean(cen * cen, axis=-1, keepdims=True)
        y = cen * lax.rsqrt(var + LN_EPS) * lng_ref[...] + lnb_ref[...]
        y = _silu(y).astype(BF16)
        o_ref[0, rows_p, :] = x_ref[0, rows_p, :] + _dot(y, wout_ref[...]) + bout_ref[...]


def _slabs(v):
    rows, d = v.shape
    return v.reshape(rows, d // LANES, LANES).transpose(1, 0, 2)


def _conv_mixer(x, g, w_in, b_in, w_dw, b_dw, ln_g, ln_b, w_out, b_out, layer):
    b, s, d = x.shape
    tm = CONV_TILE
    row = lambda v: v.reshape(1, -1)
    tile = pl.BlockSpec((1, tm, d), lambda i, j: (i, j, 0))
    taps = _slabs(jnp.repeat(w_dw.astype(BF16), BF16_ROWS, axis=0))
    n_slabs = d // LANES
    return pl.pallas_call(
        functools.partial(_conv_body, tm=tm, d=d),
        grid=(b, s // tm),
        in_specs=[
            tile,
            _const_spec((1, d)),
            _layer_spec(w_in.shape, layer),
            _const_spec((1, 2 * d)),
            _const_spec(taps.shape),
            _const_spec((n_slabs, 1, LANES)),
            _const_spec((1, d)),
            _const_spec((1, d)),
            _layer_spec(w_out.shape, layer),
            _const_spec((1, d)),
        ],
        out_specs=tile,
        out_shape=jax.ShapeDtypeStruct((b, s, d), F32),
        scratch_shapes=[
            pltpu.VMEM((n_slabs, CONV_HALO + tm + 8, LANES), F32),
            pltpu.VMEM((n_slabs, (CONV_HALO + tm) // 2, LANES), jnp.uint32),
            pltpu.VMEM((n_slabs, (CONV_HALO + tm) // 2, LANES), jnp.uint32),
            pltpu.VMEM((n_slabs, tm, LANES), F32),
        ],
        compiler_params=_params(2),
        name="conv_mixer",
    )(x, row(g), w_in, row(b_in), taps, _slabs(row(b_dw)), row(ln_g), row(ln_b), w_out, row(b_out))


def _pool_body(x_ref, g_ref, w_ref, sc_ref, o_ref, hbuf, *, tm, d):
    j = pl.program_id(1)

    n_slabs = d // LANES

    @pl.when(j == 0)
    def _():
        hbuf[:, 0:POOL_HALO, :] = jnp.zeros((n_slabs, POOL_HALO, LANES), F32)

    x = x_ref[0]
    h = _rms(x, g_ref[...])
    for s in range(n_slabs):
        hbuf[s, POOL_HALO:POOL_HALO + tm, :] = h[:, s * LANES:(s + 1) * LANES]
    pos = j * tm + lax.broadcasted_iota(jnp.int32, (tm, 1), 0) + 1
    per_group = n_slabs // len(POOL_WINDOWS)
    outs = []
    for gi, w in enumerate(POOL_WINDOWS):
        cnt = jnp.minimum(pos, w).astype(F32)
        cols = []
        for s in range(gi * per_group, (gi + 1) * per_group):
            cur = h[:, s * LANES:(s + 1) * LANES]
            tot = cur
            for k in range(1, w):
                tot = tot + hbuf[s, pl.ds(POOL_HALO - k, tm, stride=1), :]
            cols.append(tot / cnt - cur)
        p = jnp.concatenate(cols, axis=-1)
        outs.append(_dot(p.astype(BF16), w_ref[gi]))
    for s in range(n_slabs):
        hbuf[s, 0:POOL_HALO, :] = hbuf[s, tm:tm + POOL_HALO, :]
    o_ref[0] = x + jnp.concatenate(outs, axis=-1) * sc_ref[...]


def _pool_mixer(x, g, w_grp, scale):
    b, s, d = x.shape
    tm = POOL_TILE
    tile = pl.BlockSpec((1, tm, d), lambda i, j: (i, j, 0))
    return pl.pallas_call(
        functools.partial(_pool_body, tm=tm, d=d),
        grid=(b, s // tm),
        in_specs=[tile, _const_spec((1, d)), _const_spec(w_grp.shape), _const_spec((1, d))],
        out_specs=tile,
        out_shape=jax.ShapeDtypeStruct((b, s, d), F32),
        scratch_shapes=[pltpu.VMEM((d // LANES, tm + POOL_HALO, LANES), F32)],
        compiler_params=_params(2),
        name="pool_mixer",
    )(x, g.reshape(1, d), w_grp, scale.reshape(1, d))


def _qkv_body(x_ref, g_ref, w_ref, cos_ref, sin_ref, *rest, tm):
    outs, slab = rest[:-1], rest[-1]
    h = _rms(x_ref[0], g_ref[...]).astype(BF16)
    cos = cos_ref[...]
    sin = sin_ref[...]
    lane = lax.broadcasted_iota(jnp.int32, (tm, LANES), 1)
    first_half = (lane % HEAD_DIM) < HEAD_DIM // 2
    n_slabs = GROUP_WIDTH // LANES
    for kind in range(3):
        for gi, (_, dil) in enumerate(ATTN_GROUPS):
            c0 = (kind * len(ATTN_GROUPS) + gi) * GROUP_WIDTH
            y = _dot(h, w_ref[:, c0:c0 + GROUP_WIDTH])
            out = outs[gi]
            kind_lo = kind * dil * GROUP_WIDTH
            for s in range(n_slabs):
                ys = y[:, s * LANES:(s + 1) * LANES]
                if kind < 2:
                    half = HEAD_DIM // 2
                    swapped = jnp.where(first_half,
                                        pltpu.roll(ys, LANES - half, axis=1),
                                        pltpu.roll(ys, half, axis=1))
                    ys = ys * cos + swapped * sin
                if kind == 0:
                    ys = ys * Q_SCALE
                if dil == 1:
                    out[0, :, kind_lo + s * LANES:kind_lo + (s + 1) * LANES] = ys.astype(BF16)
                else:
                    slab[s] = ys
            if dil > 1:
                rows = tm // dil
                for r in range(dil):
                    for s in range(n_slabs):
                        lo = kind_lo + r * GROUP_WIDTH + s * LANES
                        out[0, :, lo:lo + LANES] = slab[s, pl.ds(r, rows, stride=dil), :].astype(BF16)


def _rope_tables(s):
    half = HEAD_DIM // 2
    inv_freq = ROPE_THETA ** (-jnp.arange(half, dtype=F32) / half)
    ang = jnp.arange(s, dtype=F32)[:, None] * inv_freq[None, :]
    cos = jnp.tile(jnp.cos(ang), (1, LANES // half))
    sin = jnp.sin(ang)
    sin = jnp.tile(jnp.concatenate([-sin, sin], axis=-1), (1, LANES // HEAD_DIM))
    return cos, sin


def _qkv_rope(x, g, w_qkv):
    b, s, d = x.shape
    tm = QKV_TILE
    cos, sin = _rope_tables(s)
    out_specs, out_shapes = [], []
    for _, dil in ATTN_GROUPS:
        out_specs.append(pl.BlockSpec((1, tm // dil, 3 * dil * GROUP_WIDTH), lambda j, i: (i, j, 0)))
        out_shapes.append(jax.ShapeDtypeStruct((b, s // dil, 3 * dil * GROUP_WIDTH), BF16))
    return pl.pallas_call(
        functools.partial(_qkv_body, tm=tm),
        grid=(s // tm, b),
        in_specs=[
            pl.BlockSpec((1, tm, d), lambda j, i: (i, j, 0)),
            _const_spec((1, d)),
            _const_spec(w_qkv.shape),
            pl.BlockSpec((tm, LANES), lambda j, i: (j, 0)),
            pl.BlockSpec((tm, LANES), lambda j, i: (j, 0)),
        ],
        out_specs=out_specs,
        out_shape=out_shapes,
        scratch_shapes=[pltpu.VMEM((GROUP_WIDTH // LANES, tm, LANES), F32)],
        compiler_params=_params(2),
        name="qkv_rope",
    )(x, g.reshape(1, d), w_qkv, cos, sin)


def _attn_body(q_ref, k_ref, v_ref, bias_ref, o_ref, st_ref, *, n, residues):
    blk = ATTN_BLOCK
    lane = lax.broadcasted_iota(jnp.int32, (blk, LANES), 1)
    head0 = lane < HEAD_DIM
    stat_head = lane // STAT_LANES
    ones = jnp.ones((2 * blk, LANES), BF16)

    def one_block(ib, carry):
        row0 = pl.multiple_of(ib * blk, blk)
        prow0 = pl.multiple_of(jnp.maximum(ib - 1, 0) * blk, blk)
        bias = bias_ref[jnp.minimum(ib, 1)]
        for r in range(residues):
            one_residue(r, row0, prow0, bias)
        return carry

    def one_residue(r, row0, prow0, bias):
        stats = jnp.zeros((blk, LANES), F32)
        for p in range(GROUP_WIDTH // LANES):
            lanes = slice(r * GROUP_WIDTH + p * LANES, r * GROUP_WIDTH + (p + 1) * LANES)
            q2 = q_ref[0, pl.ds(row0, blk), lanes]
            k2 = jnp.concatenate([k_ref[0, pl.ds(prow0, blk), lanes],
                                  k_ref[0, pl.ds(row0, blk), lanes]], axis=0)
            v2 = jnp.concatenate([v_ref[0, pl.ds(prow0, blk), lanes],
                                  v_ref[0, pl.ds(row0, blk), lanes]], axis=0)
            zero = jnp.zeros_like(q2)
            qq = jnp.concatenate([jnp.where(head0, q2, zero), jnp.where(head0, zero, q2)], axis=0)
            sc = lax.dot_general(qq, k2, (((1,), (1,)), ((), ())), preferred_element_type=F32) + bias
            m = jnp.max(sc, axis=-1, keepdims=True)
            pe = jnp.exp2(sc - m).astype(BF16)
            pvd = _dot(pe, jnp.concatenate([v2, ones], axis=1))
            den = pvd[:, LANES:]
            pv = pvd[:, :LANES] * (1.0 / den)
            o_ref[0, pl.ds(row0, blk), lanes] = jnp.where(head0, pv[:blk], pv[blk:]).astype(BF16)
            lse = m + jnp.log2(den)
            stats = jnp.where(stat_head == 2 * p, lse[:blk], stats)
            stats = jnp.where(stat_head == 2 * p + 1, lse[blk:], stats)
        st_ref[0, r, pl.ds(row0, blk), :] = stats

    lax.fori_loop(0, n // blk, one_block, 0, unroll=min(max(1, ATTN_UNROLL // residues), n // blk))


def _attn_bias():
    blk = ATTN_BLOCK
    qi = jnp.arange(2 * blk)[:, None] % blk
    kj = jnp.arange(2 * blk)[None, :]
    band = (kj >= qi) & (kj <= qi + blk)
    first = band & (kj >= blk)
    return jnp.where(jnp.stack([first, band]), 0.0, NEG_INF).astype(F32)


def _attn_group(qkv, dil):
    b, n, _ = qkv.shape
    residues = min(dil, ATTN_UNROLL)
    steps = dil // residues
    width = residues * GROUP_WIDTH
    seq = pl.BlockSpec((1, n, width), lambda i, r: (i, 0, r))
    kinds = [pl.BlockSpec((1, n, width), functools.partial(lambda i, r, kind: (i, 0, kind * steps + r), kind=kind))
             for kind in range(3)]
    bias = _attn_bias()
    return pl.pallas_call(
        functools.partial(_attn_body, n=n, residues=residues),
        grid=(b, steps),
        in_specs=kinds + [_const_spec(bias.shape)],
        out_specs=[seq, pl.BlockSpec((1, residues, n, LANES), lambda i, r: (i, r, 0, 0))],
        out_shape=[jax.ShapeDtypeStruct((b, n, dil * GROUP_WIDTH), BF16),
                   jax.ShapeDtypeStruct((b, dil, n, LANES), F32)],
        compiler_params=_params(2),
        name=f"attn_dil{dil}",
    )(qkv, qkv, qkv, bias)


def _merge_body(*refs, tm):
    ng = len(ATTN_GROUPS)
    o_refs, st_refs = refs[:ng], refs[ng:2 * ng]
    ex_ref, out_ref, oscr, sscr = refs[2 * ng:]
    n_slabs = GROUP_WIDTH // LANES
    vals, lses = [], []
    for gi, (_, dil) in enumerate(ATTN_GROUPS):
        if dil == 1:
            vals.append(o_refs[gi][0].astype(F32))
            lses.append(st_refs[gi][0, 0])
            continue
        rows = tm // dil
        for r in range(dil):
            for s in range(n_slabs):
                lo = r * GROUP_WIDTH + s * LANES
                oscr[gi, s, pl.ds(r, rows, stride=dil), :] = o_refs[gi][0, :, lo:lo + LANES].astype(F32)
            sscr[gi, pl.ds(r, rows, stride=dil), :] = st_refs[gi][0, r]
        vals.append(jnp.concatenate([oscr[gi, s] for s in range(n_slabs)], axis=-1))
        lses.append(sscr[gi])
    top = functools.reduce(jnp.maximum, lses)
    es = [jnp.exp2(l - top) for l in lses]
    inv = 1.0 / functools.reduce(lambda a, c: a + c, es)
    o = None
    for e, val in zip(es, vals):
        alpha = e * inv
        hi = alpha.astype(BF16)
        lo = (alpha - hi.astype(F32)).astype(BF16)
        wide = _dot(hi, ex_ref[...]) + _dot(lo, ex_ref[...])
        o = wide * val if o is None else o + wide * val
    out_ref[0] = o.astype(BF16)


def _merge_groups(os_, sts):
    b = os_[0].shape[0]
    s = os_[0].shape[1] * os_[0].shape[2] // GROUP_WIDTH
    tm = MERGE_TILE
    ng = len(ATTN_GROUPS)
    src = jnp.arange(LANES)[:, None]
    dst = jnp.arange(GROUP_WIDTH)[None, :]
    expand = (src == (dst // HEAD_DIM) * STAT_LANES).astype(BF16)
    in_specs = []
    for _, dil in ATTN_GROUPS:
        in_specs.append(pl.BlockSpec((1, tm // dil, dil * GROUP_WIDTH), lambda i, j: (i, j, 0)))
    for _, dil in ATTN_GROUPS:
        in_specs.append(pl.BlockSpec((1, dil, tm // dil, LANES), lambda i, j: (i, 0, j, 0)))
    in_specs.append(_const_spec(expand.shape))
    return pl.pallas_call(
        functools.partial(_merge_body, tm=tm),
        grid=(b, s // tm),
        in_specs=in_specs,
        out_specs=pl.BlockSpec((1, tm, GROUP_WIDTH), lambda i, j: (i, j, 0)),
        out_shape=jax.ShapeDtypeStruct((b, s, GROUP_WIDTH), BF16),
        scratch_shapes=[pltpu.VMEM((ng, GROUP_WIDTH // LANES, tm, LANES), F32),
                        pltpu.VMEM((ng, tm, LANES), F32)],
        compiler_params=_params(2),
        name="attn_merge",
    )(*os_, *sts, expand)


def _attn_heads(x, g, w_qkv):
    ng = len(ATTN_GROUPS)
    qkv = _qkv_rope(x, g, w_qkv)
    os_, sts = [], []
    for gi, (_, dil) in enumerate(ATTN_GROUPS):
        o, st = _attn_group(qkv[gi], dil)
        os_.append(o)
        sts.append(st)
    return _merge_groups(os_, sts)


def kernel(x, norm_mix_g, norm_ffn_g, final_norm_g, conv_w_in, conv_b_in, conv_w_dw, conv_b_dw,
           conv_ln_g, conv_ln_b, conv_w_out, conv_b_out, attn_w_qkv, attn_w_o, pool_w, pool_scale,
           ffn_w_gate, ffn_w_up, ffn_w_down):
    b, s, d = x.shape
    depth = norm_mix_g.shape[0]
    bf = lambda w: w.astype(BF16)
    conv_w_in, conv_w_out = bf(conv_w_in), bf(conv_w_out)
    ffn_w_gate, ffn_w_up, ffn_w_down = bf(ffn_w_gate), bf(ffn_w_up), bf(ffn_w_down)
    for i in range(depth):
        kind, j = i % 3, i // 3
        heads = w_o = None
        if kind == 0:
            x = _conv_mixer(x, norm_mix_g[i], conv_w_in, conv_b_in[j], conv_w_dw[j], conv_b_dw[j],
                            conv_ln_g[j], conv_ln_b[j], conv_w_out, conv_b_out[j], layer=j)
        elif kind == 1:
            heads = _attn_heads(x, norm_mix_g[i], bf(attn_w_qkv[j])).reshape(b * s, GROUP_WIDTH)
            w_o = bf(attn_w_o[j])
        else:
            x = _pool_mixer(x, norm_mix_g[i], bf(pool_w[j]), pool_scale[j])
        last = i == depth - 1
        x = _ffn(x.reshape(b * s, d), norm_ffn_g[i], ffn_w_gate, ffn_w_up, ffn_w_down, layer=i,
                 final_g=final_norm_g if last else None, heads=heads, w_o=w_o).reshape(b, s, d)
    return x
```

```python
import functools

import jax
import jax.numpy as jnp
from jax import lax
from jax.experimental import pallas as pl
from jax.experimental.pallas import tpu as pltpu

F32 = jnp.float32
BF16 = jnp.bfloat16

RMS_EPS = 1e-6
LN_EPS = 1e-5

CONV_WIDTH = 31
CONV_HALO = 32
CONV_ROWS = 128

ATTN_GROUPS = ((128, 1), (512, 4), (2048, 16))
HEADS = 8
HEAD_DIM = 64
GROUP_WIDTH = HEADS * HEAD_DIM
ATTN_BLOCK = 128
ATTN_UNROLL = 8
LOG2_E = 1.4426950408889634
Q_SCALE = HEAD_DIM ** -0.5 * LOG2_E
ROPE_THETA = 10000.0
NEG_INF = -1e30

POOL_WINDOWS = (2, 4, 8, 16)
POOL_HALO = 16

LANES = 128
BF16_ROWS = 16
STAT_LANES = LANES // HEADS
VMEM_LIMIT_BYTES = 56 * 1024 * 1024

CONV_TILE = 1024
CONV_PARTS = 1
QKV_TILE = 1024
MERGE_TILE = 1024
POOL_TILE = 1024
FFN_TILE = 1024
FFN_CHUNK = 1024


def _const_spec(shape):
    nd = len(shape)
    return pl.BlockSpec(shape, lambda *_: (0,) * nd, pipeline_mode=pl.Buffered(1))


def _layer_spec(shape, layer):
    nd = len(shape)
    return pl.BlockSpec((pl.Squeezed(),) + tuple(shape[1:]), lambda *_: (layer,) + (0,) * (nd - 1),
                        pipeline_mode=pl.Buffered(1))


def _params(n_axes):
    return pltpu.CompilerParams(
        dimension_semantics=("arbitrary",) * n_axes,
        vmem_limit_bytes=VMEM_LIMIT_BYTES,
    )


def _rms(x, g):
    return x * lax.rsqrt(jnp.mean(x * x, axis=-1, keepdims=True) + RMS_EPS) * g


def _dot(a, b):
    return jnp.dot(a, b, preferred_element_type=F32)


def _silu(x):
    return x * jax.nn.sigmoid(x)


def _ffn_body(*refs, chunks, final, mixed):
    x_ref, g_ref, wg_ref, wu_ref, wd_ref = refs[:5]
    rest = list(refs[5:])
    o_ref = rest.pop()
    x = x_ref[...]
    if mixed:
        heads_ref, wo_ref = rest.pop(0), rest.pop(0)
        x = x + _dot(heads_ref[...], wo_ref[...])
    fg_ref = rest.pop(0) if final else None
    h = _rms(x, g_ref[...]).astype(BF16)
    acc = None
    for c0, cw in chunks:
        gate = _dot(h, wg_ref[:, c0:c0 + cw])
        up = _dot(h, wu_ref[:, c0:c0 + cw])
        act = (_silu(gate) * up).astype(BF16)
        part = _dot(act, wd_ref[c0:c0 + cw, :])
        acc = part if acc is None else acc + part
    y = x + acc
    if final:
        y = _rms(y, fg_ref[...])
    o_ref[...] = y


def _hidden_chunks(f):
    chunks, c0 = [], 0
    while c0 < f:
        cw = min(FFN_CHUNK, f - c0)
        chunks.append((c0, cw))
        c0 += cw
    return tuple(chunks)


def _ffn(x2d, g, wg, wu, wd, layer, final_g=None, heads=None, w_o=None):
    n, d = x2d.shape
    f = wg.shape[2]
    tm = FFN_TILE
    final = final_g is not None
    in_specs = [
        pl.BlockSpec((tm, d), lambda i: (i, 0)),
        _const_spec((1, d)),
        _layer_spec(wg.shape, layer),
        _layer_spec(wu.shape, layer),
        _layer_spec(wd.shape, layer),
    ]
    args = [x2d, g.reshape(1, d), wg, wu, wd]
    mixed = heads is not None
    if mixed:
        in_specs += [pl.BlockSpec((tm, heads.shape[1]), lambda i: (i, 0)), _const_spec(w_o.shape)]
        args += [heads, w_o]
    if final:
        in_specs.append(_const_spec((1, d)))
        args.append(final_g.reshape(1, d))
    return pl.pallas_call(
        functools.partial(_ffn_body, chunks=_hidden_chunks(f), final=final, mixed=mixed),
        grid=(n // tm,),
        in_specs=in_specs,
        out_specs=pl.BlockSpec((tm, d), lambda i: (i, 0)),
        out_shape=jax.ShapeDtypeStruct((n, d), F32),
        compiler_params=_params(1),
        name="ffn",
    )(*args)


def _conv_body(x_ref, g_ref, win_ref, bin_ref, wdw_ref, bdw_ref, lng_ref, lnb_ref,
               wout_ref, bout_ref, o_ref, ubuf, even, odd, cbuf, *, tm, d):
    n_slabs = d // LANES
    rows = CONV_ROWS
    span = CONV_HALO + tm

    @pl.when(pl.program_id(1) == 0)
    def _():
        ubuf[:, 0:CONV_HALO, :] = jnp.zeros((n_slabs, CONV_HALO, LANES), F32)
        ubuf[:, span:span + 8, :] = jnp.zeros((n_slabs, 8, LANES), F32)

    part = tm // CONV_PARTS
    for p in range(CONV_PARTS):
        h = _rms(x_ref[0, p * part:(p + 1) * part, :], g_ref[...]).astype(BF16)
        a = _dot(h, win_ref[:, 0:d]) + bin_ref[:, 0:d]
        gate = _dot(h, win_ref[:, d:2 * d]) + bin_ref[:, d:2 * d]
        u = a * jax.nn.sigmoid(gate)
        for s in range(n_slabs):
            ubuf[s, CONV_HALO + p * part:CONV_HALO + (p + 1) * part, :] = u[:, s * LANES:(s + 1) * LANES]
    for s in range(n_slabs):
        even[s] = pltpu.bitcast(ubuf[s, 0:span, :].astype(BF16), jnp.uint32)
        odd[s] = pltpu.bitcast(ubuf[s, pl.ds(1, span, stride=1), :].astype(BF16), jnp.uint32)

    first = CONV_HALO - (CONV_WIDTH - 1)

    def conv_rows(c, carry):
        base = pl.multiple_of(c * rows, rows)
        word = pl.multiple_of(c * (rows // 2), rows // 2)
        for s in range(n_slabs):
            acc = jnp.zeros((rows, LANES), F32)
            for k in range(CONV_WIDTH):
                off = first + k
                src = even if off % 2 == 0 else odd
                words = src[s, pl.ds(word + off // 2, rows // 2, stride=1), :]
                tap = wdw_ref[s, BF16_ROWS * k:BF16_ROWS * (k + 1), :].astype(F32)
                tap = jnp.concatenate([tap] * (rows // BF16_ROWS), axis=0)
                acc = acc + pltpu.bitcast(words, BF16).astype(F32) * tap
            cbuf[s, pl.ds(base, rows), :] = acc + bdw_ref[s]
        return carry

    lax.fori_loop(0, tm // rows, conv_rows, 0)
    for s in range(n_slabs):
        ubuf[s, 0:CONV_HALO, :] = ubuf[s, tm:span, :]

    for p in range(CONV_PARTS):
        rows_p = slice(p * part, (p + 1) * part)
        cv = jnp.concatenate([cbuf[s, rows_p, :] for s in range(n_slabs)], axis=-1)
        mu = jnp.mean(cv, axis=-1, keepdims=True)
        cen = cv - mu
        var = jnp.mean(cen * cen, axis=-1, keepdims=True)
        y = cen * lax.rsqrt(var + LN_EPS) * lng_ref[...] + lnb_ref[...]
        y = _silu(y).astype(BF16)
        o_ref[0, rows_p, :] = x_ref[0, rows_p, :] + _dot(y, wout_ref[...]) + bout_ref[...]


def _slabs(v):
    rows, d = v.shape
    return v.reshape(rows, d // LANES, LANES).transpose(1, 0, 2)


def _conv_mixer(x, g, w_in, b_in, w_dw, b_dw, ln_g, ln_b, w_out, b_out, layer):
    b, s, d = x.shape
    tm = CONV_TILE
    row = lambda v: v.reshape(1, -1)
    tile = pl.BlockSpec((1, tm, d), lambda i, j: (i, j, 0))
    taps = _slabs(jnp.repeat(w_dw.astype(BF16), BF16_ROWS, axis=0))
    n_slabs = d // LANES
    return pl.pallas_call(
        functools.partial(_conv_body, tm=tm, d=d),
        grid=(b, s // tm),
        in_specs=[
            tile,
            _const_spec((1, d)),
            _layer_spec(w_in.shape, layer),
            _const_spec((1, 2 * d)),
            _const_spec(taps.shape),
            _const_spec((n_slabs, 1, LANES)),
            _const_spec((1, d)),
            _const_spec((1, d)),
            _layer_spec(w_out.shape, layer),
            _const_spec((1, d)),
        ],
        out_specs=tile,
        out_shape=jax.ShapeDtypeStruct((b, s, d), F32),
        scratch_shapes=[
            pltpu.VMEM((n_slabs, CONV_HALO + tm + 8, LANES), F32),
            pltpu.VMEM((n_slabs, (CONV_HALO + tm) // 2, LANES), jnp.uint32),
            pltpu.VMEM((n_slabs, (CONV_HALO + tm) // 2, LANES), jnp.uint32),
            pltpu.VMEM((n_slabs, tm, LANES), F32),
        ],
        compiler_params=_params(2),
        name="conv_mixer",
    )(x, row(g), w_in, row(b_in), taps, _slabs(row(b_dw)), row(ln_g), row(ln_b), w_out, row(b_out))


def _pool_body(x_ref, g_ref, w_ref, sc_ref, o_ref, hbuf, *, tm, d):
    j = pl.program_id(1)

    n_slabs = d // LANES

    @pl.when(j == 0)
    def _():
        hbuf[:, 0:POOL_HALO, :] = jnp.zeros((n_slabs, POOL_HALO, LANES), F32)

    x = x_ref[0]
    h = _rms(x, g_ref[...])
    for s in range(n_slabs):
        hbuf[s, POOL_HALO:POOL_HALO + tm, :] = h[:, s * LANES:(s + 1) * LANES]
    pos = j * tm + lax.broadcasted_iota(jnp.int32, (tm, 1), 0) + 1
    per_group = n_slabs // len(POOL_WINDOWS)
    outs = []
    for gi, w in enumerate(POOL_WINDOWS):
        cnt = jnp.minimum(pos, w).astype(F32)
        cols = []
        for s in range(gi * per_group, (gi + 1) * per_group):
            cur = h[:, s * LANES:(s + 1) * LANES]
            tot = cur
            for k in range(1, w):
                tot = tot + hbuf[s, pl.ds(POOL_HALO - k, tm, stride=1), :]
            cols.append(tot / cnt - cur)
        p = jnp.concatenate(cols, axis=-1)
        outs.append(_dot(p.astype(BF16), w_ref[gi]))
    for s in range(n_slabs):
        hbuf[s, 0:POOL_HALO, :] = hbuf[s, tm:tm + POOL_HALO, :]
    o_ref[0] = x + jnp.concatenate(outs, axis=-1) * sc_ref[...]


def _pool_mixer(x, g, w_grp, scale):
    b, s, d = x.shape
    tm = POOL_TILE
    tile = pl.BlockSpec((1, tm, d), lambda i, j: (i, j, 0))
    return pl.pallas_call(
        functools.partial(_pool_body, tm=tm, d=d),
        grid=(b, s // tm),
        in_specs=[tile, _const_spec((1, d)), _const_spec(w_grp.shape), _const_spec((1, d))],
        out_specs=tile,
        out_shape=jax.ShapeDtypeStruct((b, s, d), F32),
        scratch_shapes=[pltpu.VMEM((d // LANES, tm + POOL_HALO, LANES), F32)],
        compiler_params=_params(2),
        name="pool_mixer",
    )(x, g.reshape(1, d), w_grp, scale.reshape(1, d))


def _qkv_body(x_ref, g_ref, w_ref, cos_ref, sin_ref, *rest, tm):
    outs, hslab = rest[:-1], rest[-1]
    hn = _rms(x_ref[0], g_ref[...])
    d_model = hn.shape[1]
    for s in range(d_model // LANES):
        hslab[s] = hn[:, s * LANES:(s + 1) * LANES]
    lhs = {}
    for _, dil in ATTN_GROUPS:
        if dil == 1:
            lhs[dil] = hn.astype(BF16)
            continue
        rows = tm // dil
        parts = [jnp.concatenate([hslab[s, pl.ds(r, rows, stride=dil), :] for s in range(d_model // LANES)], axis=-1)
                 for r in range(dil)]
        lhs[dil] = jnp.concatenate(parts, axis=0).astype(BF16)
    lane = lax.broadcasted_iota(jnp.int32, (tm, LANES), 1)
    first_half = (lane % HEAD_DIM) < HEAD_DIM // 2
    half = HEAD_DIM // 2
    for kind in range(3):
        for gi, (_, dil) in enumerate(ATTN_GROUPS):
            c0 = (kind * len(ATTN_GROUPS) + gi) * GROUP_WIDTH
            y = _dot(lhs[dil], w_ref[:, c0:c0 + GROUP_WIDTH])
            out = outs[gi]
            kind_lo = kind * dil * GROUP_WIDTH
            rows = tm // dil
            for s in range(GROUP_WIDTH // LANES):
                ys = y[:, s * LANES:(s + 1) * LANES]
                if kind < 2:
                    swapped = jnp.where(first_half,
                                        pltpu.roll(ys, LANES - half, axis=1),
                                        pltpu.roll(ys, half, axis=1))
                    ys = ys * cos_ref[gi] + swapped * sin_ref[gi]
                if kind == 0:
                    ys = ys * Q_SCALE
                ys = ys.astype(BF16)
                for r in range(dil):
                    lo = kind_lo + r * GROUP_WIDTH + s * LANES
                    out[0, :, lo:lo + LANES] = ys[r * rows:(r + 1) * rows]


def _rope_tables(s, tm):
    half = HEAD_DIM // 2
    inv_freq = ROPE_THETA ** (-jnp.arange(half, dtype=F32) / half)
    ang = jnp.arange(s, dtype=F32)[:, None] * inv_freq[None, :]
    cos = jnp.tile(jnp.cos(ang), (1, LANES // half))
    sin = jnp.sin(ang)
    sin = jnp.tile(jnp.concatenate([-sin, sin], axis=-1), (1, LANES // HEAD_DIM))

    def reorder(t, dil):
        return t.reshape(s // tm, tm // dil, dil, LANES).transpose(0, 2, 1, 3).reshape(s, LANES)

    return (jnp.stack([reorder(cos, dil) for _, dil in ATTN_GROUPS]),
            jnp.stack([reorder(sin, dil) for _, dil in ATTN_GROUPS]))


def _qkv_rope(x, g, w_qkv):
    b, s, d = x.shape
    tm = QKV_TILE
    ng = len(ATTN_GROUPS)
    cos, sin = _rope_tables(s, tm)
    out_specs, out_shapes = [], []
    for _, dil in ATTN_GROUPS:
        out_specs.append(pl.BlockSpec((1, tm // dil, 3 * dil * GROUP_WIDTH), lambda j, i: (i, j, 0)))
        out_shapes.append(jax.ShapeDtypeStruct((b, s // dil, 3 * dil * GROUP_WIDTH), BF16))
    return pl.pallas_call(
        functools.partial(_qkv_body, tm=tm),
        grid=(s // tm, b),
        in_specs=[
            pl.BlockSpec((1, tm, d), lambda j, i: (i, j, 0)),
            _const_spec((1, d)),
            _const_spec(w_qkv.shape),
            pl.BlockSpec((ng, tm, LANES), lambda j, i: (0, j, 0)),
            pl.BlockSpec((ng, tm, LANES), lambda j, i: (0, j, 0)),
        ],
        out_specs=out_specs,
        out_shape=out_shapes,
        scratch_shapes=[pltpu.VMEM((d // LANES, tm, LANES), F32)],
        compiler_params=_params(2),
        name="qkv_rope",
    )(x, g.reshape(1, d), w_qkv, cos, sin)


def _attn_body(q_ref, k_ref, v_ref, bias_ref, o_ref, st_ref, *, n, residues):
    blk = ATTN_BLOCK
    lane = lax.broadcasted_iota(jnp.int32, (blk, LANES), 1)
    head0 = lane < HEAD_DIM
    stat_head = lane // STAT_LANES
    ones = jnp.ones((2 * blk, LANES), BF16)

    def one_block(ib, carry):
        row0 = pl.multiple_of(ib * blk, blk)
        prow0 = pl.multiple_of(jnp.maximum(ib - 1, 0) * blk, blk)
        bias = bias_ref[jnp.minimum(ib, 1)]
        for r in range(residues):
            one_residue(r, row0, prow0, bias)
        return carry

    def one_residue(r, row0, prow0, bias):
        stats = jnp.zeros((blk, LANES), F32)
        for p in range(GROUP_WIDTH // LANES):
            lanes = slice(r * GROUP_WIDTH + p * LANES, r * GROUP_WIDTH + (p + 1) * LANES)
            q2 = q_ref[0, pl.ds(row0, blk), lanes]
            k2 = jnp.concatenate([k_ref[0, pl.ds(prow0, blk), lanes],
                                  k_ref[0, pl.ds(row0, blk), lanes]], axis=0)
            v2 = jnp.concatenate([v_ref[0, pl.ds(prow0, blk), lanes],
                                  v_ref[0, pl.ds(row0, blk), lanes]], axis=0)
            zero = jnp.zeros_like(q2)
            qq = jnp.concatenate([jnp.where(head0, q2, zero), jnp.where(head0, zero, q2)], axis=0)
            sc = lax.dot_general(qq, k2, (((1,), (1,)), ((), ())), preferred_element_type=F32) + bias
            m = jnp.max(sc, axis=-1, keepdims=True)
            pe = jnp.exp2(sc - m).astype(BF16)
            pvd = _dot(pe, jnp.concatenate([v2, ones], axis=1))
            den = pvd[:, LANES:]
            pv = pvd[:, :LANES] * (1.0 / den)
            o_ref[0, pl.ds(row0, blk), lanes] = jnp.where(head0, pv[:blk], pv[blk:]).astype(BF16)
            lse = m + jnp.log2(den)
            stats = jnp.where(stat_head == 2 * p, lse[:blk], stats)
            stats = jnp.where(stat_head == 2 * p + 1, lse[blk:], stats)
        st_ref[0, r, pl.ds(row0, blk), :] = stats

    lax.fori_loop(0, n // blk, one_block, 0, unroll=min(max(1, ATTN_UNROLL // residues), n // blk))


def _attn_bias():
    blk = ATTN_BLOCK
    qi = jnp.arange(2 * blk)[:, None] % blk
    kj = jnp.arange(2 * blk)[None, :]
    band = (kj >= qi) & (kj <= qi + blk)
    first = band & (kj >= blk)
    return jnp.where(jnp.stack([first, band]), 0.0, NEG_INF).astype(F32)


def _attn_group(qkv, dil):
    b, n, _ = qkv.shape
    residues = min(dil, ATTN_UNROLL)
    steps = dil // residues
    width = residues * GROUP_WIDTH
    seq = pl.BlockSpec((1, n, width), lambda i, r: (i, 0, r))
    kinds = [pl.BlockSpec((1, n, width), functools.partial(lambda i, r, kind: (i, 0, kind * steps + r), kind=kind))
             for kind in range(3)]
    bias = _attn_bias()
    return pl.pallas_call(
        functools.partial(_attn_body, n=n, residues=residues),
        grid=(b, steps),
        in_specs=kinds + [_const_spec(bias.shape)],
        out_specs=[seq, pl.BlockSpec((1, residues, n, LANES), lambda i, r: (i, r, 0, 0))],
        out_shape=[jax.ShapeDtypeStruct((b, n, dil * GROUP_WIDTH), BF16),
                   jax.ShapeDtypeStruct((b, dil, n, LANES), F32)],
        compiler_params=_params(2),
        name=f"attn_dil{dil}",
    )(qkv, qkv, qkv, bias)


def _merge_body(*refs, tm):
    ng = len(ATTN_GROUPS)
    o_refs, st_refs = refs[:ng], refs[ng:2 * ng]
    ex_ref, out_ref, oscr, sscr = refs[2 * ng:]
    n_slabs = GROUP_WIDTH // LANES
    vals, lses = [], []
    for gi, (_, dil) in enumerate(ATTN_GROUPS):
        if dil == 1:
            vals.append(o_refs[gi][0].astype(F32))
            lses.append(st_refs[gi][0, 0])
            continue
        rows = tm // dil
        for r in range(dil):
            for s in range(n_slabs):
                lo = r * GROUP_WIDTH + s * LANES
                oscr[gi, s, pl.ds(r, rows, stride=dil), :] = o_refs[gi][0, :, lo:lo + LANES].astype(F32)
            sscr[gi, pl.ds(r, rows, stride=dil), :] = st_refs[gi][0, r]
        vals.append(jnp.concatenate([oscr[gi, s] for s in range(n_slabs)], axis=-1))
        lses.append(sscr[gi])
    top = functools.reduce(jnp.maximum, lses)
    es = [jnp.exp2(l - top) for l in lses]
    inv = 1.0 / functools.reduce(lambda a, c: a + c, es)
    o = None
    for e, val in zip(es, vals):
        alpha = e * inv
        hi = alpha.astype(BF16)
        lo = (alpha - hi.astype(F32)).astype(BF16)
        wide = _dot(hi, ex_ref[...]) + _dot(lo, ex_ref[...])
        o = wide * val if o is None else o + wide * val
    out_ref[0] = o.astype(BF16)


def _merge_groups(os_, sts):
    b = os_[0].shape[0]
    s = os_[0].shape[1] * os_[0].shape[2] // GROUP_WIDTH
    tm = MERGE_TILE
    ng = len(ATTN_GROUPS)
    src = jnp.arange(LANES)[:, None]
    dst = jnp.arange(GROUP_WIDTH)[None, :]
    expand = (src == (dst // HEAD_DIM) * STAT_LANES).astype(BF16)
    in_specs = []
    for _, dil in ATTN_GROUPS:
        in_specs.append(pl.BlockSpec((1, tm // dil, dil * GROUP_WIDTH), lambda i, j: (i, j, 0)))
    for _, dil in ATTN_GROUPS:
        in_specs.append(pl.BlockSpec((1, dil, tm // dil, LANES), lambda i, j: (i, 0, j, 0)))
    in_specs.append(_const_spec(expand.shape))
    return pl.pallas_call(
        functools.partial(_merge_body, tm=tm),
        grid=(b, s // tm),
        in_specs=in_specs,
        out_specs=pl.BlockSpec((1, tm, GROUP_WIDTH), lambda i, j: (i, j, 0)),
        out_shape=jax.ShapeDtypeStruct((b, s, GROUP_WIDTH), BF16),
        scratch_shapes=[pltpu.VMEM((ng, GROUP_WIDTH // LANES, tm, LANES), F32),
                        pltpu.VMEM((ng, tm, LANES), F32)],
        compiler_params=_params(2),
        name="attn_merge",
    )(*os_, *sts, expand)


def _attn_heads(x, g, w_qkv):
    ng = len(ATTN_GROUPS)
    qkv = _qkv_rope(x, g, w_qkv)
    os_, sts = [], []
    for gi, (_, dil) in enumerate(ATTN_GROUPS):
        o, st = _attn_group(qkv[gi], dil)
        os_.append(o)
        sts.append(st)
    return _merge_groups(os_, sts)


def kernel(x, norm_mix_g, norm_ffn_g, final_norm_g, conv_w_in, conv_b_in, conv_w_dw, conv_b_dw,
           conv_ln_g, conv_ln_b, conv_w_out, conv_b_out, attn_w_qkv, attn_w_o, pool_w, pool_scale,
           ffn_w_gate, ffn_w_up, ffn_w_down):
    b, s, d = x.shape
    depth = norm_mix_g.shape[0]
    bf = lambda w: w.astype(BF16)
    conv_w_in, conv_w_out = bf(conv_w_in), bf(conv_w_out)
    ffn_w_gate, ffn_w_up, ffn_w_down = bf(ffn_w_gate), bf(ffn_w_up), bf(ffn_w_down)
    for i in range(depth):
        kind, j = i % 3, i // 3
        heads = w_o = None
        if kind == 0:
            x = _conv_mixer(x, norm_mix_g[i], conv_w_in, conv_b_in[j], conv_w_dw[j], conv_b_dw[j],
                            conv_ln_g[j], conv_ln_b[j], conv_w_out, conv_b_out[j], layer=j)
        elif kind == 1:
            heads = _attn_heads(x, norm_mix_g[i], bf(attn_w_qkv[j])).reshape(b * s, GROUP_WIDTH)
            w_o = bf(attn_w_o[j])
        else:
            x = _pool_mixer(x, norm_mix_g[i], bf(pool_w[j]), pool_scale[j])
        last = i == depth - 1
        x = _ffn(x.reshape(b * s, d), norm_ffn_g[i], ffn_w_gate, ffn_w_up, ffn_w_down, layer=i,
                 final_g=final_norm_g if last else None, heads=heads, w_o=w_o).reshape(b, s, d)
    return x
```

```python
import functools

import numpy as np
import jax
import jax.numpy as jnp
from jax import lax
from jax.experimental import pallas as pl
from jax.experimental.pallas import tpu as pltpu

F32 = jnp.float32
BF16 = jnp.bfloat16

RMS_EPS = 1e-6
LN_EPS = 1e-5

CONV_WIDTH = 31
CONV_HALO = 32
CONV_ROWS = 128

ATTN_GROUPS = ((128, 1), (512, 4), (2048, 16))
HEADS = 8
HEAD_DIM = 64
GROUP_WIDTH = HEADS * HEAD_DIM
ATTN_BLOCK = 128
ATTN_UNROLL = 8
LOG2_E = 1.4426950408889634
Q_SCALE = HEAD_DIM ** -0.5 * LOG2_E
ROPE_THETA = 10000.0
NEG_INF = -1e30

POOL_WINDOWS = (2, 4, 8, 16)
POOL_HALO = 16

LANES = 128
BF16_ROWS = 16
STAT_LANES = LANES // HEADS
VMEM_LIMIT_BYTES = 56 * 1024 * 1024

CONV_TILE = 1024
CONV_PARTS = 1
QKV_TILE = 1024
MERGE_TILE = 1024
POOL_TILE = 1024
FFN_TILE = 1024
FFN_CHUNK = 1024


def _const_spec(shape):
    nd = len(shape)
    return pl.BlockSpec(shape, lambda *_: (0,) * nd, pipeline_mode=pl.Buffered(1))


def _layer_spec(shape, layer):
    nd = len(shape)
    return pl.BlockSpec((pl.Squeezed(),) + tuple(shape[1:]), lambda *_: (layer,) + (0,) * (nd - 1),
                        pipeline_mode=pl.Buffered(1))


def _params(n_axes):
    return pltpu.CompilerParams(
        dimension_semantics=("arbitrary",) * n_axes,
        vmem_limit_bytes=VMEM_LIMIT_BYTES,
    )


def _rms(x, g):
    return x * lax.rsqrt(jnp.mean(x * x, axis=-1, keepdims=True) + RMS_EPS) * g


def _dot(a, b):
    return jnp.dot(a, b, preferred_element_type=F32)


def _silu(x):
    return x * jax.nn.sigmoid(x)


def _ffn_body(*refs, chunks, final, mixed):
    x_ref, g_ref, wg_ref, wu_ref, wd_ref = refs[:5]
    rest = list(refs[5:])
    o_ref = rest.pop()
    x = x_ref[...]
    if mixed:
        heads_ref, wo_ref = rest.pop(0), rest.pop(0)
        x = x + _dot(heads_ref[...], wo_ref[...])
    fg_ref = rest.pop(0) if final else None
    h = _rms(x, g_ref[...]).astype(BF16)
    acc = None
    for c0, cw in chunks:
        gate = _dot(h, wg_ref[:, c0:c0 + cw])
        up = _dot(h, wu_ref[:, c0:c0 + cw])
        act = (_silu(gate) * up).astype(BF16)
        part = _dot(act, wd_ref[c0:c0 + cw, :])
        acc = part if acc is None else acc + part
    y = x + acc
    if final:
        y = _rms(y, fg_ref[...])
    o_ref[...] = y


def _hidden_chunks(f):
    chunks, c0 = [], 0
    while c0 < f:
        cw = min(FFN_CHUNK, f - c0)
        chunks.append((c0, cw))
        c0 += cw
    return tuple(chunks)


def _ffn(x2d, g, wg, wu, wd, layer, final_g=None, heads=None, w_o=None):
    n, d = x2d.shape
    f = wg.shape[2]
    tm = FFN_TILE
    final = final_g is not None
    in_specs = [
        pl.BlockSpec((tm, d), lambda i: (i, 0)),
        _const_spec((1, d)),
        _layer_spec(wg.shape, layer),
        _layer_spec(wu.shape, layer),
        _layer_spec(wd.shape, layer),
    ]
    args = [x2d, g.reshape(1, d), wg, wu, wd]
    mixed = heads is not None
    if mixed:
        in_specs += [pl.BlockSpec((tm, heads.shape[1]), lambda i: (i, 0)), _const_spec(w_o.shape)]
        args += [heads, w_o]
    if final:
        in_specs.append(_const_spec((1, d)))
        args.append(final_g.reshape(1, d))
    return pl.pallas_call(
        functools.partial(_ffn_body, chunks=_hidden_chunks(f), final=final, mixed=mixed),
        grid=(n // tm,),
        in_specs=in_specs,
        out_specs=pl.BlockSpec((tm, d), lambda i: (i, 0)),
        out_shape=jax.ShapeDtypeStruct((n, d), F32),
        compiler_params=_params(1),
        name="ffn",
    )(*args)


def _conv_body(x_ref, g_ref, win_ref, bin_ref, wdw_ref, bdw_ref, lng_ref, lnb_ref,
               wout_ref, bout_ref, o_ref, ubuf, even, odd, cbuf, *, tm, d):
    n_slabs = d // LANES
    rows = CONV_ROWS
    span = CONV_HALO + tm

    @pl.when(pl.program_id(1) == 0)
    def _():
        ubuf[:, 0:CONV_HALO, :] = jnp.zeros((n_slabs, CONV_HALO, LANES), F32)
        ubuf[:, span:span + 8, :] = jnp.zeros((n_slabs, 8, LANES), F32)

    part = tm // CONV_PARTS
    for p in range(CONV_PARTS):
        h = _rms(x_ref[0, p * part:(p + 1) * part, :], g_ref[...]).astype(BF16)
        a = _dot(h, win_ref[:, 0:d]) + bin_ref[:, 0:d]
        gate = _dot(h, win_ref[:, d:2 * d]) + bin_ref[:, d:2 * d]
        u = a * jax.nn.sigmoid(gate)
        for s in range(n_slabs):
            ubuf[s, CONV_HALO + p * part:CONV_HALO + (p + 1) * part, :] = u[:, s * LANES:(s + 1) * LANES]
    for s in range(n_slabs):
        even[s] = pltpu.bitcast(ubuf[s, 0:span, :].astype(BF16), jnp.uint32)
        odd[s] = pltpu.bitcast(ubuf[s, pl.ds(1, span, stride=1), :].astype(BF16), jnp.uint32)

    first = CONV_HALO - (CONV_WIDTH - 1)

    def conv_rows(c, carry):
        base = pl.multiple_of(c * rows, rows)
        word = pl.multiple_of(c * (rows // 2), rows // 2)
        for s in range(n_slabs):
            acc = jnp.zeros((rows, LANES), F32)
            for k in range(CONV_WIDTH):
                off = first + k
                src = even if off % 2 == 0 else odd
                words = src[s, pl.ds(word + off // 2, rows // 2, stride=1), :]
                tap = wdw_ref[s, BF16_ROWS * k:BF16_ROWS * (k + 1), :].astype(F32)
                tap = jnp.concatenate([tap] * (rows // BF16_ROWS), axis=0)
                acc = acc + pltpu.bitcast(words, BF16).astype(F32) * tap
            cbuf[s, pl.ds(base, rows), :] = acc + bdw_ref[s]
        return carry

    lax.fori_loop(0, tm // rows, conv_rows, 0)
    for s in range(n_slabs):
        ubuf[s, 0:CONV_HALO, :] = ubuf[s, tm:span, :]

    for p in range(CONV_PARTS):
        rows_p = slice(p * part, (p + 1) * part)
        cv = jnp.concatenate([cbuf[s, rows_p, :] for s in range(n_slabs)], axis=-1)
        mu = jnp.mean(cv, axis=-1, keepdims=True)
        cen = cv - mu
        var = jnp.mean(cen * cen, axis=-1, keepdims=True)
        y = cen * lax.rsqrt(var + LN_EPS) * lng_ref[...] + lnb_ref[...]
        y = _silu(y).astype(BF16)
        o_ref[0, rows_p, :] = x_ref[0, rows_p, :] + _dot(y, wout_ref[...]) + bout_ref[...]


def _slabs(v):
    rows, d = v.shape
    return v.reshape(rows, d // LANES, LANES).transpose(1, 0, 2)


def _conv_mixer(x, g, w_in, b_in, w_dw, b_dw, ln_g, ln_b, w_out, b_out, layer):
    b, s, d = x.shape
    tm = CONV_TILE
    row = lambda v: v.reshape(1, -1)
    tile = pl.BlockSpec((1, tm, d), lambda i, j: (i, j, 0))
    taps = _slabs(jnp.repeat(w_dw.astype(BF16), BF16_ROWS, axis=0))
    n_slabs = d // LANES
    return pl.pallas_call(
        functools.partial(_conv_body, tm=tm, d=d),
        grid=(b, s // tm),
        in_specs=[
            tile,
            _const_spec((1, d)),
            _layer_spec(w_in.shape, layer),
            _const_spec((1, 2 * d)),
            _const_spec(taps.shape),
            _const_spec((n_slabs, 1, LANES)),
            _const_spec((1, d)),
            _const_spec((1, d)),
            _layer_spec(w_out.shape, layer),
            _const_spec((1, d)),
        ],
        out_specs=tile,
        out_shape=jax.ShapeDtypeStruct((b, s, d), F32),
        scratch_shapes=[
            pltpu.VMEM((n_slabs, CONV_HALO + tm + 8, LANES), F32),
            pltpu.VMEM((n_slabs, (CONV_HALO + tm) // 2, LANES), jnp.uint32),
            pltpu.VMEM((n_slabs, (CONV_HALO + tm) // 2, LANES), jnp.uint32),
            pltpu.VMEM((n_slabs, tm, LANES), F32),
        ],
        compiler_params=_params(2),
        name="conv_mixer",
    )(x, row(g), w_in, row(b_in), taps, _slabs(row(b_dw)), row(ln_g), row(ln_b), w_out, row(b_out))


def _pool_body(x_ref, g_ref, w_ref, sc_ref, o_ref, hbuf, *, tm, d):
    j = pl.program_id(1)

    n_slabs = d // LANES

    @pl.when(j == 0)
    def _():
        hbuf[:, 0:POOL_HALO, :] = jnp.zeros((n_slabs, POOL_HALO, LANES), F32)

    x = x_ref[0]
    h = _rms(x, g_ref[...])
    for s in range(n_slabs):
        hbuf[s, POOL_HALO:POOL_HALO + tm, :] = h[:, s * LANES:(s + 1) * LANES]
    pos = j * tm + lax.broadcasted_iota(jnp.int32, (tm, 1), 0) + 1
    per_group = n_slabs // len(POOL_WINDOWS)
    outs = []
    for gi, w in enumerate(POOL_WINDOWS):
        cnt = jnp.minimum(pos, w).astype(F32)
        cols = []
        for s in range(gi * per_group, (gi + 1) * per_group):
            cur = h[:, s * LANES:(s + 1) * LANES]
            tot = cur
            for k in range(1, w):
                tot = tot + hbuf[s, pl.ds(POOL_HALO - k, tm, stride=1), :]
            cols.append(tot / cnt - cur)
        p = jnp.concatenate(cols, axis=-1)
        outs.append(_dot(p.astype(BF16), w_ref[gi]))
    for s in range(n_slabs):
        hbuf[s, 0:POOL_HALO, :] = hbuf[s, tm:tm + POOL_HALO, :]
    o_ref[0] = x + jnp.concatenate(outs, axis=-1) * sc_ref[...]


def _pool_mixer(x, g, w_grp, scale):
    b, s, d = x.shape
    tm = POOL_TILE
    tile = pl.BlockSpec((1, tm, d), lambda i, j: (i, j, 0))
    return pl.pallas_call(
        functools.partial(_pool_body, tm=tm, d=d),
        grid=(b, s // tm),
        in_specs=[tile, _const_spec((1, d)), _const_spec(w_grp.shape), _const_spec((1, d))],
        out_specs=tile,
        out_shape=jax.ShapeDtypeStruct((b, s, d), F32),
        scratch_shapes=[pltpu.VMEM((d // LANES, tm + POOL_HALO, LANES), F32)],
        compiler_params=_params(2),
        name="pool_mixer",
    )(x, g.reshape(1, d), w_grp, scale.reshape(1, d))


def _qkv_body(x_ref, g_ref, w_ref, cos_ref, sin_ref, *rest, tm):
    outs, hslab = rest[:-1], rest[-1]
    hn = _rms(x_ref[0], g_ref[...])
    d_model = hn.shape[1]
    for s in range(d_model // LANES):
        hslab[s] = hn[:, s * LANES:(s + 1) * LANES]
    lhs = {}
    for _, dil in ATTN_GROUPS:
        if dil == 1:
            lhs[dil] = hn.astype(BF16)
            continue
        rows = tm // dil
        parts = [jnp.concatenate([hslab[s, pl.ds(r, rows, stride=dil), :] for s in range(d_model // LANES)], axis=-1)
                 for r in range(dil)]
        lhs[dil] = jnp.concatenate(parts, axis=0).astype(BF16)
    lane = lax.broadcasted_iota(jnp.int32, (tm, LANES), 1)
    first_half = (lane % HEAD_DIM) < HEAD_DIM // 2
    half = HEAD_DIM // 2
    for kind in range(3):
        for gi, (_, dil) in enumerate(ATTN_GROUPS):
            c0 = (kind * len(ATTN_GROUPS) + gi) * GROUP_WIDTH
            y = _dot(lhs[dil], w_ref[:, c0:c0 + GROUP_WIDTH])
            out = outs[gi]
            kind_lo = kind * dil * GROUP_WIDTH
            rows = tm // dil
            for s in range(GROUP_WIDTH // LANES):
                ys = y[:, s * LANES:(s + 1) * LANES]
                if kind < 2:
                    swapped = jnp.where(first_half,
                                        pltpu.roll(ys, LANES - half, axis=1),
                                        pltpu.roll(ys, half, axis=1))
                    ys = ys * cos_ref[gi] + swapped * sin_ref[gi]
                if kind == 0:
                    ys = ys * Q_SCALE
                ys = ys.astype(BF16)
                for r in range(dil):
                    lo = kind_lo + r * GROUP_WIDTH + s * LANES
                    out[0, :, lo:lo + LANES] = ys[r * rows:(r + 1) * rows]


def _rope_tables(s, tm):
    half = HEAD_DIM // 2
    inv_freq = np.float32(ROPE_THETA) ** (-np.arange(half, dtype=np.float32) / np.float32(half))
    ang = (np.arange(s, dtype=np.float32)[:, None] * inv_freq[None, :]).astype(np.float64)
    cos = np.tile(np.cos(ang), (1, LANES // half))
    sin = np.sin(ang)
    sin = np.tile(np.concatenate([-sin, sin], axis=-1), (1, LANES // HEAD_DIM))

    def reorder(t, dil):
        return t.reshape(s // tm, tm // dil, dil, LANES).transpose(0, 2, 1, 3).reshape(s, LANES)

    return (jnp.asarray(np.stack([reorder(cos, dil) for _, dil in ATTN_GROUPS]), F32),
            jnp.asarray(np.stack([reorder(sin, dil) for _, dil in ATTN_GROUPS]), F32))


def _qkv_rope(x, g, w_qkv):
    b, s, d = x.shape
    tm = QKV_TILE
    ng = len(ATTN_GROUPS)
    cos, sin = _rope_tables(s, tm)
    out_specs, out_shapes = [], []
    for _, dil in ATTN_GROUPS:
        out_specs.append(pl.BlockSpec((1, tm // dil, 3 * dil * GROUP_WIDTH), lambda j, i: (i, j, 0)))
        out_shapes.append(jax.ShapeDtypeStruct((b, s // dil, 3 * dil * GROUP_WIDTH), BF16))
    return pl.pallas_call(
        functools.partial(_qkv_body, tm=tm),
        grid=(s // tm, b),
        in_specs=[
            pl.BlockSpec((1, tm, d), lambda j, i: (i, j, 0)),
            _const_spec((1, d)),
            _const_spec(w_qkv.shape),
            pl.BlockSpec((ng, tm, LANES), lambda j, i: (0, j, 0)),
            pl.BlockSpec((ng, tm, LANES), lambda j, i: (0, j, 0)),
        ],
        out_specs=out_specs,
        out_shape=out_shapes,
        scratch_shapes=[pltpu.VMEM((d // LANES, tm, LANES), F32)],
        compiler_params=_params(2),
        name="qkv_rope",
    )(x, g.reshape(1, d), w_qkv, cos, sin)


def _attn_body(q_ref, k_ref, v_ref, bias_ref, o_ref, st_ref, *, n, residues):
    blk = ATTN_BLOCK
    lane = lax.broadcasted_iota(jnp.int32, (blk, LANES), 1)
    head0 = lane < HEAD_DIM
    stat_head = lane // STAT_LANES
    ones = jnp.ones((2 * blk, LANES), BF16)

    def one_block(ib, carry):
        row0 = pl.multiple_of(ib * blk, blk)
        prow0 = pl.multiple_of(jnp.maximum(ib - 1, 0) * blk, blk)
        bias = bias_ref[jnp.minimum(ib, 1)]
        for r in range(residues):
            one_residue(r, row0, prow0, bias)
        return carry

    def one_residue(r, row0, prow0, bias):
        stats = jnp.zeros((blk, LANES), F32)
        for p in range(GROUP_WIDTH // LANES):
            lanes = slice(r * GROUP_WIDTH + p * LANES, r * GROUP_WIDTH + (p + 1) * LANES)
            q2 = q_ref[0, pl.ds(row0, blk), lanes]
            k2 = jnp.concatenate([k_ref[0, pl.ds(prow0, blk), lanes],
                                  k_ref[0, pl.ds(row0, blk), lanes]], axis=0)
            v2 = jnp.concatenate([v_ref[0, pl.ds(prow0, blk), lanes],
                                  v_ref[0, pl.ds(row0, blk), lanes]], axis=0)
            zero = jnp.zeros_like(q2)
            qq = jnp.concatenate([jnp.where(head0, q2, zero), jnp.where(head0, zero, q2)], axis=0)
            sc = lax.dot_general(qq, k2, (((1,), (1,)), ((), ())), preferred_element_type=F32) + bias
            m = jnp.max(sc, axis=-1, keepdims=True)
            pe = jnp.exp2(sc - m).astype(BF16)
            pvd = _dot(pe, jnp.concatenate([v2, ones], axis=1))
            den = pvd[:, LANES:]
            pv = pvd[:, :LANES] * (1.0 / den)
            o_ref[0, pl.ds(row0, blk), lanes] = jnp.where(head0, pv[:blk], pv[blk:]).astype(BF16)
            lse = m + jnp.log2(den)
            stats = jnp.where(stat_head == 2 * p, lse[:blk], stats)
            stats = jnp.where(stat_head == 2 * p + 1, lse[blk:], stats)
        st_ref[0, r, pl.ds(row0, blk), :] = stats

    lax.fori_loop(0, n // blk, one_block, 0, unroll=min(max(1, ATTN_UNROLL // residues), n // blk))


def _attn_bias():
    blk = ATTN_BLOCK
    qi = np.arange(2 * blk)[:, None] % blk
    kj = np.arange(2 * blk)[None, :]
    band = (kj >= qi) & (kj <= qi + blk)
    first = band & (kj >= blk)
    return jnp.asarray(np.where(np.stack([first, band]), 0.0, NEG_INF), F32)


def _attn_group(qkv, dil):
    b, n, _ = qkv.shape
    residues = min(dil, ATTN_UNROLL)
    steps = dil // residues
    width = residues * GROUP_WIDTH
    seq = pl.BlockSpec((1, n, width), lambda i, r: (i, 0, r))
    kinds = [pl.BlockSpec((1, n, width), functools.partial(lambda i, r, kind: (i, 0, kind * steps + r), kind=kind))
             for kind in range(3)]
    bias = _attn_bias()
    return pl.pallas_call(
        functools.partial(_attn_body, n=n, residues=residues),
        grid=(b, steps),
        in_specs=kinds + [_const_spec(bias.shape)],
        out_specs=[seq, pl.BlockSpec((1, residues, n, LANES), lambda i, r: (i, r, 0, 0))],
        out_shape=[jax.ShapeDtypeStruct((b, n, dil * GROUP_WIDTH), BF16),
                   jax.ShapeDtypeStruct((b, dil, n, LANES), F32)],
        compiler_params=_params(2),
        name=f"attn_dil{dil}",
    )(qkv, qkv, qkv, bias)


def _merge_body(*refs, tm):
    ng = len(ATTN_GROUPS)
    o_refs, st_refs = refs[:ng], refs[ng:2 * ng]
    ex_ref, out_ref, oscr, sscr = refs[2 * ng:]
    n_slabs = GROUP_WIDTH // LANES
    vals, lses = [], []
    for gi, (_, dil) in enumerate(ATTN_GROUPS):
        if dil == 1:
            vals.append(o_refs[gi][0].astype(F32))
            lses.append(st_refs[gi][0, 0])
            continue
        rows = tm // dil
        for r in range(dil):
            for s in range(n_slabs):
                lo = r * GROUP_WIDTH + s * LANES
                oscr[gi, s, pl.ds(r, rows, stride=dil), :] = o_refs[gi][0, :, lo:lo + LANES].astype(F32)
            sscr[gi, pl.ds(r, rows, stride=dil), :] = st_refs[gi][0, r]
        vals.append(jnp.concatenate([oscr[gi, s] for s in range(n_slabs)], axis=-1))
        lses.append(sscr[gi])
    top = functools.reduce(jnp.maximum, lses)
    es = [jnp.exp2(l - top) for l in lses]
    inv = 1.0 / functools.reduce(lambda a, c: a + c, es)
    o = None
    for e, val in zip(es, vals):
        alpha = e * inv
        hi = alpha.astype(BF16)
        lo = (alpha - hi.astype(F32)).astype(BF16)
        wide = _dot(hi, ex_ref[...]) + _dot(lo, ex_ref[...])
        o = wide * val if o is None else o + wide * val
    out_ref[0] = o.astype(BF16)


def _merge_groups(os_, sts):
    b = os_[0].shape[0]
    s = os_[0].shape[1] * os_[0].shape[2] // GROUP_WIDTH
    tm = MERGE_TILE
    ng = len(ATTN_GROUPS)
    src = np.arange(LANES)[:, None]
    dst = np.arange(GROUP_WIDTH)[None, :]
    expand = jnp.asarray(src == (dst // HEAD_DIM) * STAT_LANES, BF16)
    in_specs = []
    for _, dil in ATTN_GROUPS:
        in_specs.append(pl.BlockSpec((1, tm // dil, dil * GROUP_WIDTH), lambda i, j: (i, j, 0)))
    for _, dil in ATTN_GROUPS:
        in_specs.append(pl.BlockSpec((1, dil, tm // dil, LANES), lambda i, j: (i, 0, j, 0)))
    in_specs.append(_const_spec(expand.shape))
    return pl.pallas_call(
        functools.partial(_merge_body, tm=tm),
        grid=(b, s // tm),
        in_specs=in_specs,
        out_specs=pl.BlockSpec((1, tm, GROUP_WIDTH), lambda i, j: (i, j, 0)),
        out_shape=jax.ShapeDtypeStruct((b, s, GROUP_WIDTH), BF16),
        scratch_shapes=[pltpu.VMEM((ng, GROUP_WIDTH // LANES, tm, LANES), F32),
                        pltpu.VMEM((ng, tm, LANES), F32)],
        compiler_params=_params(2),
        name="attn_merge",
    )(*os_, *sts, expand)


def _attn_heads(x, g, w_qkv):
    ng = len(ATTN_GROUPS)
    qkv = _qkv_rope(x, g, w_qkv)
    os_, sts = [], []
    for gi, (_, dil) in enumerate(ATTN_GROUPS):
        o, st = _attn_group(qkv[gi], dil)
        os_.append(o)
        sts.append(st)
    return _merge_groups(os_, sts)


def kernel(x, norm_mix_g, norm_ffn_g, final_norm_g, conv_w_in, conv_b_in, conv_w_dw, conv_b_dw,
           conv_ln_g, conv_ln_b, conv_w_out, conv_b_out, attn_w_qkv, attn_w_o, pool_w, pool_scale,
           ffn_w_gate, ffn_w_up, ffn_w_down):
    b, s, d = x.shape
    depth = norm_mix_g.shape[0]
    bf = lambda w: w.astype(BF16)
    conv_w_in, conv_w_out = bf(conv_w_in), bf(conv_w_out)
    ffn_w_gate, ffn_w_up, ffn_w_down = bf(ffn_w_gate), bf(ffn_w_up), bf(ffn_w_down)
    for i in range(depth):
        kind, j = i % 3, i // 3
        heads = w_o = None
        if kind == 0:
            x = _conv_mixer(x, norm_mix_g[i], conv_w_in, conv_b_in[j], conv_w_dw[j], conv_b_dw[j],
                            conv_ln_g[j], conv_ln_b[j], conv_w_out, conv_b_out[j], layer=j)
        elif kind == 1:
            heads = _attn_heads(x, norm_mix_g[i], bf(attn_w_qkv[j])).reshape(b * s, GROUP_WIDTH)
            w_o = bf(attn_w_o[j])
        else:
            x = _pool_mixer(x, norm_mix_g[i], bf(pool_w[j]), pool_scale[j])
        last = i == depth - 1
        x = _ffn(x.reshape(b * s, d), norm_ffn_g[i], ffn_w_gate, ffn_w_up, ffn_w_down, layer=i,
                 final_g=final_norm_g if last else None, heads=heads, w_o=w_o).reshape(b, s, d)
    return x
```

```python
import functools

import numpy as np
import jax
import jax.numpy as jnp
from jax import lax
from jax.experimental import pallas as pl
from jax.experimental.pallas import tpu as pltpu

F32 = jnp.float32
BF16 = jnp.bfloat16

RMS_EPS = 1e-6
LN_EPS = 1e-5

CONV_WIDTH = 31
CONV_HALO = 32
CONV_ROWS = 128

ATTN_GROUPS = ((128, 1), (512, 4), (2048, 16))
HEADS = 8
HEAD_DIM = 64
GROUP_WIDTH = HEADS * HEAD_DIM
ATTN_BLOCK = 128
ATTN_UNROLL = 16
LOG2_E = 1.4426950408889634
Q_SCALE = HEAD_DIM ** -0.5 * LOG2_E
ROPE_THETA = 10000.0
NEG_INF = -1e30

POOL_WINDOWS = (2, 4, 8, 16)
POOL_HALO = 16

LANES = 128
BF16_ROWS = 16
STAT_LANES = LANES // HEADS
VMEM_LIMIT_BYTES = 56 * 1024 * 1024

CONV_TILE = 1024
CONV_PARTS = 1
QKV_TILE = 1024
MERGE_TILE = 1024
POOL_TILE = 1024
FFN_TILE = 1024
FFN_CHUNK = 1024


def _const_spec(shape):
    nd = len(shape)
    return pl.BlockSpec(shape, lambda *_: (0,) * nd, pipeline_mode=pl.Buffered(1))


def _layer_spec(shape, layer):
    nd = len(shape)
    return pl.BlockSpec((pl.Squeezed(),) + tuple(shape[1:]), lambda *_: (layer,) + (0,) * (nd - 1),
                        pipeline_mode=pl.Buffered(1))


def _params(n_axes):
    return pltpu.CompilerParams(
        dimension_semantics=("arbitrary",) * n_axes,
        vmem_limit_bytes=VMEM_LIMIT_BYTES,
    )


def _rms(x, g):
    return x * lax.rsqrt(jnp.mean(x * x, axis=-1, keepdims=True) + RMS_EPS) * g


def _dot(a, b):
    return jnp.dot(a, b, preferred_element_type=F32)


def _silu(x):
    return x * jax.nn.sigmoid(x)


def _ffn_body(*refs, chunks, final, mixed):
    x_ref, g_ref, wg_ref, wu_ref, wd_ref = refs[:5]
    rest = list(refs[5:])
    o_ref = rest.pop()
    x = x_ref[...]
    if mixed:
        heads_ref, wo_ref = rest.pop(0), rest.pop(0)
        x = x + _dot(heads_ref[...], wo_ref[...])
    fg_ref = rest.pop(0) if final else None
    h = _rms(x, g_ref[...]).astype(BF16)
    acc = None
    for c0, cw in chunks:
        gate = _dot(h, wg_ref[:, c0:c0 + cw])
        up = _dot(h, wu_ref[:, c0:c0 + cw])
        act = (_silu(gate) * up).astype(BF16)
        part = _dot(act, wd_ref[c0:c0 + cw, :])
        acc = part if acc is None else acc + part
    y = x + acc
    if final:
        y = _rms(y, fg_ref[...])
    o_ref[...] = y


def _hidden_chunks(f):
    chunks, c0 = [], 0
    while c0 < f:
        cw = min(FFN_CHUNK, f - c0)
        chunks.append((c0, cw))
        c0 += cw
    return tuple(chunks)


def _ffn(x2d, g, wg, wu, wd, layer, final_g=None, heads=None, w_o=None):
    n, d = x2d.shape
    f = wg.shape[2]
    tm = FFN_TILE
    final = final_g is not None
    in_specs = [
        pl.BlockSpec((tm, d), lambda i: (i, 0)),
        _const_spec((1, d)),
        _layer_spec(wg.shape, layer),
        _layer_spec(wu.shape, layer),
        _layer_spec(wd.shape, layer),
    ]
    args = [x2d, g.reshape(1, d), wg, wu, wd]
    mixed = heads is not None
    if mixed:
        in_specs += [pl.BlockSpec((tm, heads.shape[1]), lambda i: (i, 0)), _const_spec(w_o.shape)]
        args += [heads, w_o]
    if final:
        in_specs.append(_const_spec((1, d)))
        args.append(final_g.reshape(1, d))
    return pl.pallas_call(
        functools.partial(_ffn_body, chunks=_hidden_chunks(f), final=final, mixed=mixed),
        grid=(n // tm,),
        in_specs=in_specs,
        out_specs=pl.BlockSpec((tm, d), lambda i: (i, 0)),
        out_shape=jax.ShapeDtypeStruct((n, d), F32),
        compiler_params=_params(1),
        name="ffn",
    )(*args)


def _conv_body(x_ref, g_ref, win_ref, bin_ref, wdw_ref, bdw_ref, lng_ref, lnb_ref,
               wout_ref, bout_ref, o_ref, ubuf, even, odd, cbuf, *, tm, d):
    n_slabs = d // LANES
    rows = CONV_ROWS
    span = CONV_HALO + tm

    @pl.when(pl.program_id(1) == 0)
    def _():
        ubuf[:, 0:CONV_HALO, :] = jnp.zeros((n_slabs, CONV_HALO, LANES), F32)
        ubuf[:, span:span + 8, :] = jnp.zeros((n_slabs, 8, LANES), F32)

    part = tm // CONV_PARTS
    for p in range(CONV_PARTS):
        h = _rms(x_ref[0, p * part:(p + 1) * part, :], g_ref[...]).astype(BF16)
        a = _dot(h, win_ref[:, 0:d]) + bin_ref[:, 0:d]
        gate = _dot(h, win_ref[:, d:2 * d]) + bin_ref[:, d:2 * d]
        u = a * jax.nn.sigmoid(gate)
        for s in range(n_slabs):
            ubuf[s, CONV_HALO + p * part:CONV_HALO + (p + 1) * part, :] = u[:, s * LANES:(s + 1) * LANES]
    for s in range(n_slabs):
        even[s] = pltpu.bitcast(ubuf[s, 0:span, :].astype(BF16), jnp.uint32)
        odd[s] = pltpu.bitcast(ubuf[s, pl.ds(1, span, stride=1), :].astype(BF16), jnp.uint32)

    first = CONV_HALO - (CONV_WIDTH - 1)

    def conv_rows(c, carry):
        base = pl.multiple_of(c * rows, rows)
        word = pl.multiple_of(c * (rows // 2), rows // 2)
        for s in range(n_slabs):
            acc = jnp.zeros((rows, LANES), F32)
            for k in range(CONV_WIDTH):
                off = first + k
                src = even if off % 2 == 0 else odd
                words = src[s, pl.ds(word + off // 2, rows // 2, stride=1), :]
                tap = wdw_ref[s, BF16_ROWS * k:BF16_ROWS * (k + 1), :].astype(F32)
                tap = jnp.concatenate([tap] * (rows // BF16_ROWS), axis=0)
                acc = acc + pltpu.bitcast(words, BF16).astype(F32) * tap
            cbuf[s, pl.ds(base, rows), :] = acc + bdw_ref[s]
        return carry

    lax.fori_loop(0, tm // rows, conv_rows, 0)
    for s in range(n_slabs):
        ubuf[s, 0:CONV_HALO, :] = ubuf[s, tm:span, :]

    for p in range(CONV_PARTS):
        rows_p = slice(p * part, (p + 1) * part)
        cv = jnp.concatenate([cbuf[s, rows_p, :] for s in range(n_slabs)], axis=-1)
        mu = jnp.mean(cv, axis=-1, keepdims=True)
        cen = cv - mu
        var = jnp.mean(cen * cen, axis=-1, keepdims=True)
        y = cen * lax.rsqrt(var + LN_EPS) * lng_ref[...] + lnb_ref[...]
        y = _silu(y).astype(BF16)
        o_ref[0, rows_p, :] = x_ref[0, rows_p, :] + _dot(y, wout_ref[...]) + bout_ref[...]


def _slabs(v):
    rows, d = v.shape
    return v.reshape(rows, d // LANES, LANES).transpose(1, 0, 2)


def _conv_mixer(x, g, w_in, b_in, w_dw, b_dw, ln_g, ln_b, w_out, b_out, layer):
    b, s, d = x.shape
    tm = CONV_TILE
    row = lambda v: v.reshape(1, -1)
    tile = pl.BlockSpec((1, tm, d), lambda i, j: (i, j, 0))
    taps = _slabs(jnp.repeat(w_dw.astype(BF16), BF16_ROWS, axis=0))
    n_slabs = d // LANES
    return pl.pallas_call(
        functools.partial(_conv_body, tm=tm, d=d),
        grid=(b, s // tm),
        in_specs=[
            tile,
            _const_spec((1, d)),
            _layer_spec(w_in.shape, layer),
            _const_spec((1, 2 * d)),
            _const_spec(taps.shape),
            _const_spec((n_slabs, 1, LANES)),
            _const_spec((1, d)),
            _const_spec((1, d)),
            _layer_spec(w_out.shape, layer),
            _const_spec((1, d)),
        ],
        out_specs=tile,
        out_shape=jax.ShapeDtypeStruct((b, s, d), F32),
        scratch_shapes=[
            pltpu.VMEM((n_slabs, CONV_HALO + tm + 8, LANES), F32),
            pltpu.VMEM((n_slabs, (CONV_HALO + tm) // 2, LANES), jnp.uint32),
            pltpu.VMEM((n_slabs, (CONV_HALO + tm) // 2, LANES), jnp.uint32),
            pltpu.VMEM((n_slabs, tm, LANES), F32),
        ],
        compiler_params=_params(2),
        name="conv_mixer",
    )(x, row(g), w_in, row(b_in), taps, _slabs(row(b_dw)), row(ln_g), row(ln_b), w_out, row(b_out))


def _pool_body(x_ref, g_ref, w_ref, sc_ref, o_ref, hbuf, *, tm, d):
    j = pl.program_id(1)

    n_slabs = d // LANES

    @pl.when(j == 0)
    def _():
        hbuf[:, 0:POOL_HALO, :] = jnp.zeros((n_slabs, POOL_HALO, LANES), F32)

    x = x_ref[0]
    h = _rms(x, g_ref[...])
    for s in range(n_slabs):
        hbuf[s, POOL_HALO:POOL_HALO + tm, :] = h[:, s * LANES:(s + 1) * LANES]
    pos = j * tm + lax.broadcasted_iota(jnp.int32, (tm, 1), 0) + 1
    per_group = n_slabs // len(POOL_WINDOWS)
    outs = []
    for gi, w in enumerate(POOL_WINDOWS):
        cnt = jnp.minimum(pos, w).astype(F32)
        cols = []
        for s in range(gi * per_group, (gi + 1) * per_group):
            cur = h[:, s * LANES:(s + 1) * LANES]
            tot = cur
            for k in range(1, w):
                tot = tot + hbuf[s, pl.ds(POOL_HALO - k, tm, stride=1), :]
            cols.append(tot / cnt - cur)
        p = jnp.concatenate(cols, axis=-1)
        outs.append(_dot(p.astype(BF16), w_ref[gi]))
    for s in range(n_slabs):
        hbuf[s, 0:POOL_HALO, :] = hbuf[s, tm:tm + POOL_HALO, :]
    o_ref[0] = x + jnp.concatenate(outs, axis=-1) * sc_ref[...]


def _pool_mixer(x, g, w_grp, scale):
    b, s, d = x.shape
    tm = POOL_TILE
    tile = pl.BlockSpec((1, tm, d), lambda i, j: (i, j, 0))
    return pl.pallas_call(
        functools.partial(_pool_body, tm=tm, d=d),
        grid=(b, s // tm),
        in_specs=[tile, _const_spec((1, d)), _const_spec(w_grp.shape), _const_spec((1, d))],
        out_specs=tile,
        out_shape=jax.ShapeDtypeStruct((b, s, d), F32),
        scratch_shapes=[pltpu.VMEM((d // LANES, tm + POOL_HALO, LANES), F32)],
        compiler_params=_params(2),
        name="pool_mixer",
    )(x, g.reshape(1, d), w_grp, scale.reshape(1, d))


def _qkv_body(x_ref, g_ref, w_ref, cos_ref, sin_ref, *rest, tm):
    outs, hslab = rest[:-1], rest[-1]
    hn = _rms(x_ref[0], g_ref[...])
    d_model = hn.shape[1]
    for s in range(d_model // LANES):
        hslab[s] = hn[:, s * LANES:(s + 1) * LANES]
    lhs = {}
    for _, dil in ATTN_GROUPS:
        if dil == 1:
            lhs[dil] = hn.astype(BF16)
            continue
        rows = tm // dil
        parts = [jnp.concatenate([hslab[s, pl.ds(r, rows, stride=dil), :] for s in range(d_model // LANES)], axis=-1)
                 for r in range(dil)]
        lhs[dil] = jnp.concatenate(parts, axis=0).astype(BF16)
    lane = lax.broadcasted_iota(jnp.int32, (tm, LANES), 1)
    first_half = (lane % HEAD_DIM) < HEAD_DIM // 2
    half = HEAD_DIM // 2
    for kind in range(3):
        for gi, (_, dil) in enumerate(ATTN_GROUPS):
            c0 = (kind * len(ATTN_GROUPS) + gi) * GROUP_WIDTH
            y = _dot(lhs[dil], w_ref[:, c0:c0 + GROUP_WIDTH])
            out = outs[gi]
            kind_lo = kind * dil * GROUP_WIDTH
            rows = tm // dil
            for s in range(GROUP_WIDTH // LANES):
                ys = y[:, s * LANES:(s + 1) * LANES]
                if kind < 2:
                    swapped = jnp.where(first_half,
                                        pltpu.roll(ys, LANES - half, axis=1),
                                        pltpu.roll(ys, half, axis=1))
                    ys = ys * cos_ref[gi] + swapped * sin_ref[gi]
                if kind == 0:
                    ys = ys * Q_SCALE
                ys = ys.astype(BF16)
                for r in range(dil):
                    lo = kind_lo + r * GROUP_WIDTH + s * LANES
                    out[0, :, lo:lo + LANES] = ys[r * rows:(r + 1) * rows]


def _rope_tables(s, tm):
    half = HEAD_DIM // 2
    inv_freq = np.float32(ROPE_THETA) ** (-np.arange(half, dtype=np.float32) / np.float32(half))
    ang = (np.arange(s, dtype=np.float32)[:, None] * inv_freq[None, :]).astype(np.float64)
    cos = np.tile(np.cos(ang), (1, LANES // half))
    sin = np.sin(ang)
    sin = np.tile(np.concatenate([-sin, sin], axis=-1), (1, LANES // HEAD_DIM))

    def reorder(t, dil):
        return t.reshape(s // tm, tm // dil, dil, LANES).transpose(0, 2, 1, 3).reshape(s, LANES)

    return (jnp.asarray(np.stack([reorder(cos, dil) for _, dil in ATTN_GROUPS]), F32),
            jnp.asarray(np.stack([reorder(sin, dil) for _, dil in ATTN_GROUPS]), F32))


def _qkv_rope(x, g, w_qkv):
    b, s, d = x.shape
    tm = QKV_TILE
    ng = len(ATTN_GROUPS)
    cos, sin = _rope_tables(s, tm)
    out_specs, out_shapes = [], []
    for _, dil in ATTN_GROUPS:
        out_specs.append(pl.BlockSpec((1, tm // dil, 3 * dil * GROUP_WIDTH), lambda j, i: (i, j, 0)))
        out_shapes.append(jax.ShapeDtypeStruct((b, s // dil, 3 * dil * GROUP_WIDTH), BF16))
    return pl.pallas_call(
        functools.partial(_qkv_body, tm=tm),
        grid=(s // tm, b),
        in_specs=[
            pl.BlockSpec((1, tm, d), lambda j, i: (i, j, 0)),
            _const_spec((1, d)),
            _const_spec(w_qkv.shape),
            pl.BlockSpec((ng, tm, LANES), lambda j, i: (0, j, 0)),
            pl.BlockSpec((ng, tm, LANES), lambda j, i: (0, j, 0)),
        ],
        out_specs=out_specs,
        out_shape=out_shapes,
        scratch_shapes=[pltpu.VMEM((d // LANES, tm, LANES), F32)],
        compiler_params=_params(2),
        name="qkv_rope",
    )(x, g.reshape(1, d), w_qkv, cos, sin)


def _attn_body(q_ref, k_ref, v_ref, bias_ref, o_ref, st_ref, *, n, residues):
    blk = ATTN_BLOCK
    lane = lax.broadcasted_iota(jnp.int32, (blk, LANES), 1)
    head0 = lane < HEAD_DIM
    stat_head = lane // STAT_LANES
    ones = jnp.ones((2 * blk, LANES), BF16)

    def one_block(ib, carry):
        row0 = pl.multiple_of(ib * blk, blk)
        prow0 = pl.multiple_of(jnp.maximum(ib - 1, 0) * blk, blk)
        bias = bias_ref[jnp.minimum(ib, 1)]
        for r in range(residues):
            one_residue(r, row0, prow0, bias)
        return carry

    def one_residue(r, row0, prow0, bias):
        stats = jnp.zeros((blk, LANES), F32)
        for p in range(GROUP_WIDTH // LANES):
            lanes = slice(r * GROUP_WIDTH + p * LANES, r * GROUP_WIDTH + (p + 1) * LANES)
            q2 = q_ref[0, pl.ds(row0, blk), lanes]
            k2 = jnp.concatenate([k_ref[0, pl.ds(prow0, blk), lanes],
                                  k_ref[0, pl.ds(row0, blk), lanes]], axis=0)
            v2 = jnp.concatenate([v_ref[0, pl.ds(prow0, blk), lanes],
                                  v_ref[0, pl.ds(row0, blk), lanes]], axis=0)
            zero = jnp.zeros_like(q2)
            qq = jnp.concatenate([jnp.where(head0, q2, zero), jnp.where(head0, zero, q2)], axis=0)
            sc = lax.dot_general(qq, k2, (((1,), (1,)), ((), ())), preferred_element_type=F32) + bias
            m = jnp.max(sc, axis=-1, keepdims=True)
            pe = jnp.exp2(sc - m).astype(BF16)
            pvd = _dot(pe, jnp.concatenate([v2, ones], axis=1))
            den = pvd[:, LANES:]
            pv = pvd[:, :LANES] * (1.0 / den)
            o_ref[0, pl.ds(row0, blk), lanes] = jnp.where(head0, pv[:blk], pv[blk:]).astype(BF16)
            lse = m + jnp.log2(den)
            stats = jnp.where(stat_head == 2 * p, lse[:blk], stats)
            stats = jnp.where(stat_head == 2 * p + 1, lse[blk:], stats)
        st_ref[0, r, pl.ds(row0, blk), :] = stats

    lax.fori_loop(0, n // blk, one_block, 0, unroll=min(max(1, ATTN_UNROLL // residues), n // blk))


def _attn_bias():
    blk = ATTN_BLOCK
    qi = np.arange(2 * blk)[:, None] % blk
    kj = np.arange(2 * blk)[None, :]
    band = (kj >= qi) & (kj <= qi + blk)
    first = band & (kj >= blk)
    return jnp.asarray(np.where(np.stack([first, band]), 0.0, NEG_INF), F32)


def _attn_group(qkv, dil):
    b, n, _ = qkv.shape
    residues = min(dil, ATTN_UNROLL)
    steps = dil // residues
    width = residues * GROUP_WIDTH
    seq = pl.BlockSpec((1, n, width), lambda i, r: (i, 0, r))
    kinds = [pl.BlockSpec((1, n, width), functools.partial(lambda i, r, kind: (i, 0, kind * steps + r), kind=kind))
             for kind in range(3)]
    bias = _attn_bias()
    return pl.pallas_call(
        functools.partial(_attn_body, n=n, residues=residues),
        grid=(b, steps),
        in_specs=kinds + [_const_spec(bias.shape)],
        out_specs=[seq, pl.BlockSpec((1, residues, n, LANES), lambda i, r: (i, r, 0, 0))],
        out_shape=[jax.ShapeDtypeStruct((b, n, dil * GROUP_WIDTH), BF16),
                   jax.ShapeDtypeStruct((b, dil, n, LANES), F32)],
        compiler_params=_params(2),
        name=f"attn_dil{dil}",
    )(qkv, qkv, qkv, bias)


def _merge_body(*refs, tm):
    ng = len(ATTN_GROUPS)
    o_refs, st_refs = refs[:ng], refs[ng:2 * ng]
    ex_ref, out_ref, oscr, sscr = refs[2 * ng:]
    n_slabs = GROUP_WIDTH // LANES
    vals, lses = [], []
    for gi, (_, dil) in enumerate(ATTN_GROUPS):
        if dil == 1:
            vals.append(o_refs[gi][0].astype(F32))
            lses.append(st_refs[gi][0, 0])
            continue
        rows = tm // dil
        for r in range(dil):
            for s in range(n_slabs):
                lo = r * GROUP_WIDTH + s * LANES
                oscr[gi, s, pl.ds(r, rows, stride=dil), :] = o_refs[gi][0, :, lo:lo + LANES].astype(F32)
            sscr[gi, pl.ds(r, rows, stride=dil), :] = st_refs[gi][0, r]
        vals.append(jnp.concatenate([oscr[gi, s] for s in range(n_slabs)], axis=-1))
        lses.append(sscr[gi])
    top = functools.reduce(jnp.maximum, lses)
    es = [jnp.exp2(l - top) for l in lses]
    inv = 1.0 / functools.reduce(lambda a, c: a + c, es)
    o = None
    for e, val in zip(es, vals):
        alpha = e * inv
        hi = alpha.astype(BF16)
        lo = (alpha - hi.astype(F32)).astype(BF16)
        wide = _dot(hi, ex_ref[...]) + _dot(lo, ex_ref[...])
        o = wide * val if o is None else o + wide * val
    out_ref[0] = o.astype(BF16)


def _merge_groups(os_, sts):
    b = os_[0].shape[0]
    s = os_[0].shape[1] * os_[0].shape[2] // GROUP_WIDTH
    tm = MERGE_TILE
    ng = len(ATTN_GROUPS)
    src = np.arange(LANES)[:, None]
    dst = np.arange(GROUP_WIDTH)[None, :]
    expand = jnp.asarray(src == (dst // HEAD_DIM) * STAT_LANES, BF16)
    in_specs = []
    for _, dil in ATTN_GROUPS:
        in_specs.append(pl.BlockSpec((1, tm // dil, dil * GROUP_WIDTH), lambda i, j: (i, j, 0)))
    for _, dil in ATTN_GROUPS:
        in_specs.append(pl.BlockSpec((1, dil, tm // dil, LANES), lambda i, j: (i, 0, j, 0)))
    in_specs.append(_const_spec(expand.shape))
    return pl.pallas_call(
        functools.partial(_merge_body, tm=tm),
        grid=(b, s // tm),
        in_specs=in_specs,
        out_specs=pl.BlockSpec((1, tm, GROUP_WIDTH), lambda i, j: (i, j, 0)),
        out_shape=jax.ShapeDtypeStruct((b, s, GROUP_WIDTH), BF16),
        scratch_shapes=[pltpu.VMEM((ng, GROUP_WIDTH // LANES, tm, LANES), F32),
                        pltpu.VMEM((ng, tm, LANES), F32)],
        compiler_params=_params(2),
        name="attn_merge",
    )(*os_, *sts, expand)


def _attn_heads(x, g, w_qkv):
    ng = len(ATTN_GROUPS)
    qkv = _qkv_rope(x, g, w_qkv)
    os_, sts = [], []
    for gi, (_, dil) in enumerate(ATTN_GROUPS):
        o, st = _attn_group(qkv[gi], dil)
        os_.append(o)
        sts.append(st)
    return _merge_groups(os_, sts)


def kernel(x, norm_mix_g, norm_ffn_g, final_norm_g, conv_w_in, conv_b_in, conv_w_dw, conv_b_dw,
           conv_ln_g, conv_ln_b, conv_w_out, conv_b_out, attn_w_qkv, attn_w_o, pool_w, pool_scale,
           ffn_w_gate, ffn_w_up, ffn_w_down):
    b, s, d = x.shape
    depth = norm_mix_g.shape[0]
    bf = lambda w: w.astype(BF16)
    conv_w_in, conv_w_out = bf(conv_w_in), bf(conv_w_out)
    ffn_w_gate, ffn_w_up, ffn_w_down = bf(ffn_w_gate), bf(ffn_w_up), bf(ffn_w_down)
    for i in range(depth):
        kind, j = i % 3, i // 3
        heads = w_o = None
        if kind == 0:
            x = _conv_mixer(x, norm_mix_g[i], conv_w_in, conv_b_in[j], conv_w_dw[j], conv_b_dw[j],
                            conv_ln_g[j], conv_ln_b[j], conv_w_out, conv_b_out[j], layer=j)
        elif kind == 1:
            heads = _attn_heads(x, norm_mix_g[i], bf(attn_w_qkv[j])).reshape(b * s, GROUP_WIDTH)
            w_o = bf(attn_w_o[j])
        else:
            x = _pool_mixer(x, norm_mix_g[i], bf(pool_w[j]), pool_scale[j])
        last = i == depth - 1
        x = _ffn(x.reshape(b * s, d), norm_ffn_g[i], ffn_w_gate, ffn_w_up, ffn_w_down, layer=i,
                 final_g=final_norm_g if last else None, heads=heads, w_o=w_o).reshape(b, s, d)
    return x
```

```python
import functools

import numpy as np
import jax
import jax.numpy as jnp
from jax import lax
from jax.experimental import pallas as pl
from jax.experimental.pallas import tpu as pltpu

F32 = jnp.float32
BF16 = jnp.bfloat16

RMS_EPS = 1e-6
LN_EPS = 1e-5

CONV_WIDTH = 31
CONV_HALO = 32
CONV_ROWS = 128

ATTN_GROUPS = ((128, 1), (512, 4), (2048, 16))
HEADS = 8
HEAD_DIM = 64
GROUP_WIDTH = HEADS * HEAD_DIM
ATTN_BLOCK = 128
ATTN_UNROLL = 32
LOG2_E = 1.4426950408889634
Q_SCALE = HEAD_DIM ** -0.5 * LOG2_E
ROPE_THETA = 10000.0
NEG_INF = -1e30

POOL_WINDOWS = (2, 4, 8, 16)
POOL_HALO = 16

LANES = 128
BF16_ROWS = 16
STAT_LANES = LANES // HEADS
VMEM_LIMIT_BYTES = 56 * 1024 * 1024

CONV_TILE = 1024
CONV_PARTS = 1
QKV_TILE = 1024
MERGE_TILE = 1024
POOL_TILE = 1024
FFN_TILE = 1024
FFN_CHUNK = 1024


def _const_spec(shape):
    nd = len(shape)
    return pl.BlockSpec(shape, lambda *_: (0,) * nd, pipeline_mode=pl.Buffered(1))


def _layer_spec(shape, layer):
    nd = len(shape)
    return pl.BlockSpec((pl.Squeezed(),) + tuple(shape[1:]), lambda *_: (layer,) + (0,) * (nd - 1),
                        pipeline_mode=pl.Buffered(1))


def _params(n_axes):
    return pltpu.CompilerParams(
        dimension_semantics=("arbitrary",) * n_axes,
        vmem_limit_bytes=VMEM_LIMIT_BYTES,
    )


def _rms(x, g):
    return x * lax.rsqrt(jnp.mean(x * x, axis=-1, keepdims=True) + RMS_EPS) * g


def _dot(a, b):
    return jnp.dot(a, b, preferred_element_type=F32)


def _silu(x):
    return x * jax.nn.sigmoid(x)


def _ffn_body(*refs, chunks, final, mixed):
    x_ref, g_ref, wg_ref, wu_ref, wd_ref = refs[:5]
    rest = list(refs[5:])
    o_ref = rest.pop()
    x = x_ref[...]
    if mixed:
        heads_ref, wo_ref = rest.pop(0), rest.pop(0)
        x = x + _dot(heads_ref[...], wo_ref[...])
    fg_ref = rest.pop(0) if final else None
    h = _rms(x, g_ref[...]).astype(BF16)
    acc = None
    for c0, cw in chunks:
        gate = _dot(h, wg_ref[:, c0:c0 + cw])
        up = _dot(h, wu_ref[:, c0:c0 + cw])
        act = (_silu(gate) * up).astype(BF16)
        part = _dot(act, wd_ref[c0:c0 + cw, :])
        acc = part if acc is None else acc + part
    y = x + acc
    if final:
        y = _rms(y, fg_ref[...])
    o_ref[...] = y


def _hidden_chunks(f):
    chunks, c0 = [], 0
    while c0 < f:
        cw = min(FFN_CHUNK, f - c0)
        chunks.append((c0, cw))
        c0 += cw
    return tuple(chunks)


def _ffn(x2d, g, wg, wu, wd, layer, final_g=None, heads=None, w_o=None):
    n, d = x2d.shape
    f = wg.shape[2]
    tm = FFN_TILE
    final = final_g is not None
    in_specs = [
        pl.BlockSpec((tm, d), lambda i: (i, 0)),
        _const_spec((1, d)),
        _layer_spec(wg.shape, layer),
        _layer_spec(wu.shape, layer),
        _layer_spec(wd.shape, layer),
    ]
    args = [x2d, g.reshape(1, d), wg, wu, wd]
    mixed = heads is not None
    if mixed:
        in_specs += [pl.BlockSpec((tm, heads.shape[1]), lambda i: (i, 0)), _const_spec(w_o.shape)]
        args += [heads, w_o]
    if final:
        in_specs.append(_const_spec((1, d)))
        args.append(final_g.reshape(1, d))
    return pl.pallas_call(
        functools.partial(_ffn_body, chunks=_hidden_chunks(f), final=final, mixed=mixed),
        grid=(n // tm,),
        in_specs=in_specs,
        out_specs=pl.BlockSpec((tm, d), lambda i: (i, 0)),
        out_shape=jax.ShapeDtypeStruct((n, d), F32),
        compiler_params=_params(1),
        name="ffn",
    )(*args)


def _conv_body(x_ref, g_ref, win_ref, bin_ref, wdw_ref, bdw_ref, lng_ref, lnb_ref,
               wout_ref, bout_ref, o_ref, ubuf, even, odd, cbuf, *, tm, d):
    n_slabs = d // LANES
    rows = CONV_ROWS
    span = CONV_HALO + tm

    @pl.when(pl.program_id(1) == 0)
    def _():
        ubuf[:, 0:CONV_HALO, :] = jnp.zeros((n_slabs, CONV_HALO, LANES), F32)
        ubuf[:, span:span + 8, :] = jnp.zeros((n_slabs, 8, LANES), F32)

    part = tm // CONV_PARTS
    for p in range(CONV_PARTS):
        h = _rms(x_ref[0, p * part:(p + 1) * part, :], g_ref[...]).astype(BF16)
        a = _dot(h, win_ref[:, 0:d]) + bin_ref[:, 0:d]
        gate = _dot(h, win_ref[:, d:2 * d]) + bin_ref[:, d:2 * d]
        u = a * jax.nn.sigmoid(gate)
        for s in range(n_slabs):
            ubuf[s, CONV_HALO + p * part:CONV_HALO + (p + 1) * part, :] = u[:, s * LANES:(s + 1) * LANES]
    for s in range(n_slabs):
        even[s] = pltpu.bitcast(ubuf[s, 0:span, :].astype(BF16), jnp.uint32)
        odd[s] = pltpu.bitcast(ubuf[s, pl.ds(1, span, stride=1), :].astype(BF16), jnp.uint32)

    first = CONV_HALO - (CONV_WIDTH - 1)

    def conv_rows(c, carry):
        base = pl.multiple_of(c * rows, rows)
        word = pl.multiple_of(c * (rows // 2), rows // 2)
        for s in range(n_slabs):
            acc = jnp.zeros((rows, LANES), F32)
            for k in range(CONV_WIDTH):
                off = first + k
                src = even if off % 2 == 0 else odd
                words = src[s, pl.ds(word + off // 2, rows // 2, stride=1), :]
                tap = wdw_ref[s, BF16_ROWS * k:BF16_ROWS * (k + 1), :].astype(F32)
                tap = jnp.concatenate([tap] * (rows // BF16_ROWS), axis=0)
                acc = acc + pltpu.bitcast(words, BF16).astype(F32) * tap
            cbuf[s, pl.ds(base, rows), :] = acc + bdw_ref[s]
        return carry

    lax.fori_loop(0, tm // rows, conv_rows, 0)
    for s in range(n_slabs):
        ubuf[s, 0:CONV_HALO, :] = ubuf[s, tm:span, :]

    for p in range(CONV_PARTS):
        rows_p = slice(p * part, (p + 1) * part)
        cv = jnp.concatenate([cbuf[s, rows_p, :] for s in range(n_slabs)], axis=-1)
        mu = jnp.mean(cv, axis=-1, keepdims=True)
        cen = cv - mu
        var = jnp.mean(cen * cen, axis=-1, keepdims=True)
        y = cen * lax.rsqrt(var + LN_EPS) * lng_ref[...] + lnb_ref[...]
        y = _silu(y).astype(BF16)
        o_ref[0, rows_p, :] = x_ref[0, rows_p, :] + _dot(y, wout_ref[...]) + bout_ref[...]


def _slabs(v):
    rows, d = v.shape
    return v.reshape(rows, d // LANES, LANES).transpose(1, 0, 2)


def _conv_mixer(x, g, w_in, b_in, w_dw, b_dw, ln_g, ln_b, w_out, b_out, layer):
    b, s, d = x.shape
    tm = CONV_TILE
    row = lambda v: v.reshape(1, -1)
    tile = pl.BlockSpec((1, tm, d), lambda i, j: (i, j, 0))
    taps = _slabs(jnp.repeat(w_dw.astype(BF16), BF16_ROWS, axis=0))
    n_slabs = d // LANES
    return pl.pallas_call(
        functools.partial(_conv_body, tm=tm, d=d),
        grid=(b, s // tm),
        in_specs=[
            tile,
            _const_spec((1, d)),
            _layer_spec(w_in.shape, layer),
            _const_spec((1, 2 * d)),
            _const_spec(taps.shape),
            _const_spec((n_slabs, 1, LANES)),
            _const_spec((1, d)),
            _const_spec((1, d)),
            _layer_spec(w_out.shape, layer),
            _const_spec((1, d)),
        ],
        out_specs=tile,
        out_shape=jax.ShapeDtypeStruct((b, s, d), F32),
        scratch_shapes=[
            pltpu.VMEM((n_slabs, CONV_HALO + tm + 8, LANES), F32),
            pltpu.VMEM((n_slabs, (CONV_HALO + tm) // 2, LANES), jnp.uint32),
            pltpu.VMEM((n_slabs, (CONV_HALO + tm) // 2, LANES), jnp.uint32),
            pltpu.VMEM((n_slabs, tm, LANES), F32),
        ],
        compiler_params=_params(2),
        name="conv_mixer",
    )(x, row(g), w_in, row(b_in), taps, _slabs(row(b_dw)), row(ln_g), row(ln_b), w_out, row(b_out))


def _pool_body(x_ref, g_ref, w_ref, sc_ref, o_ref, hbuf, *, tm, d):
    j = pl.program_id(1)

    n_slabs = d // LANES

    @pl.when(j == 0)
    def _():
        hbuf[:, 0:POOL_HALO, :] = jnp.zeros((n_slabs, POOL_HALO, LANES), F32)

    x = x_ref[0]
    h = _rms(x, g_ref[...])
    for s in range(n_slabs):
        hbuf[s, POOL_HALO:POOL_HALO + tm, :] = h[:, s * LANES:(s + 1) * LANES]
    pos = j * tm + lax.broadcasted_iota(jnp.int32, (tm, 1), 0) + 1
    per_group = n_slabs // len(POOL_WINDOWS)
    outs = []
    for gi, w in enumerate(POOL_WINDOWS):
        cnt = jnp.minimum(pos, w).astype(F32)
        cols = []
        for s in range(gi * per_group, (gi + 1) * per_group):
            cur = h[:, s * LANES:(s + 1) * LANES]
            tot = cur
            for k in range(1, w):
                tot = tot + hbuf[s, pl.ds(POOL_HALO - k, tm, stride=1), :]
            cols.append(tot / cnt - cur)
        p = jnp.concatenate(cols, axis=-1)
        outs.append(_dot(p.astype(BF16), w_ref[gi]))
    for s in range(n_slabs):
        hbuf[s, 0:POOL_HALO, :] = hbuf[s, tm:tm + POOL_HALO, :]
    o_ref[0] = x + jnp.concatenate(outs, axis=-1) * sc_ref[...]


def _pool_mixer(x, g, w_grp, scale):
    b, s, d = x.shape
    tm = POOL_TILE
    tile = pl.BlockSpec((1, tm, d), lambda i, j: (i, j, 0))
    return pl.pallas_call(
        functools.partial(_pool_body, tm=tm, d=d),
        grid=(b, s // tm),
        in_specs=[tile, _const_spec((1, d)), _const_spec(w_grp.shape), _const_spec((1, d))],
        out_specs=tile,
        out_shape=jax.ShapeDtypeStruct((b, s, d), F32),
        scratch_shapes=[pltpu.VMEM((d // LANES, tm + POOL_HALO, LANES), F32)],
        compiler_params=_params(2),
        name="pool_mixer",
    )(x, g.reshape(1, d), w_grp, scale.reshape(1, d))


def _qkv_body(x_ref, g_ref, w_ref, cos_ref, sin_ref, *rest, tm):
    outs, hslab = rest[:-1], rest[-1]
    hn = _rms(x_ref[0], g_ref[...])
    d_model = hn.shape[1]
    for s in range(d_model // LANES):
        hslab[s] = hn[:, s * LANES:(s + 1) * LANES]
    lhs = {}
    for _, dil in ATTN_GROUPS:
        if dil == 1:
            lhs[dil] = hn.astype(BF16)
            continue
        rows = tm // dil
        parts = [jnp.concatenate([hslab[s, pl.ds(r, rows, stride=dil), :] for s in range(d_model // LANES)], axis=-1)
                 for r in range(dil)]
        lhs[dil] = jnp.concatenate(parts, axis=0).astype(BF16)
    lane = lax.broadcasted_iota(jnp.int32, (tm, LANES), 1)
    first_half = (lane % HEAD_DIM) < HEAD_DIM // 2
    half = HEAD_DIM // 2
    for kind in range(3):
        for gi, (_, dil) in enumerate(ATTN_GROUPS):
            c0 = (kind * len(ATTN_GROUPS) + gi) * GROUP_WIDTH
            y = _dot(lhs[dil], w_ref[:, c0:c0 + GROUP_WIDTH])
            out = outs[gi]
            kind_lo = kind * dil * GROUP_WIDTH
            rows = tm // dil
            for s in range(GROUP_WIDTH // LANES):
                ys = y[:, s * LANES:(s + 1) * LANES]
                if kind < 2:
                    swapped = jnp.where(first_half,
                                        pltpu.roll(ys, LANES - half, axis=1),
                                        pltpu.roll(ys, half, axis=1))
                    ys = ys * cos_ref[gi] + swapped * sin_ref[gi]
                if kind == 0:
                    ys = ys * Q_SCALE
                ys = ys.astype(BF16)
                for r in range(dil):
                    lo = kind_lo + r * GROUP_WIDTH + s * LANES
                    out[0, :, lo:lo + LANES] = ys[r * rows:(r + 1) * rows]


def _rope_tables(s, tm):
    half = HEAD_DIM // 2
    inv_freq = np.float32(ROPE_THETA) ** (-np.arange(half, dtype=np.float32) / np.float32(half))
    ang = (np.arange(s, dtype=np.float32)[:, None] * inv_freq[None, :]).astype(np.float64)
    cos = np.tile(np.cos(ang), (1, LANES // half))
    sin = np.sin(ang)
    sin = np.tile(np.concatenate([-sin, sin], axis=-1), (1, LANES // HEAD_DIM))

    def reorder(t, dil):
        return t.reshape(s // tm, tm // dil, dil, LANES).transpose(0, 2, 1, 3).reshape(s, LANES)

    return (jnp.asarray(np.stack([reorder(cos, dil) for _, dil in ATTN_GROUPS]), F32),
            jnp.asarray(np.stack([reorder(sin, dil) for _, dil in ATTN_GROUPS]), F32))


def _qkv_rope(x, g, w_qkv):
    b, s, d = x.shape
    tm = QKV_TILE
    ng = len(ATTN_GROUPS)
    cos, sin = _rope_tables(s, tm)
    out_specs, out_shapes = [], []
    for _, dil in ATTN_GROUPS:
        out_specs.append(pl.BlockSpec((1, tm // dil, 3 * dil * GROUP_WIDTH), lambda j, i: (i, j, 0)))
        out_shapes.append(jax.ShapeDtypeStruct((b, s // dil, 3 * dil * GROUP_WIDTH), BF16))
    return pl.pallas_call(
        functools.partial(_qkv_body, tm=tm),
        grid=(s // tm, b),
        in_specs=[
            pl.BlockSpec((1, tm, d), lambda j, i: (i, j, 0)),
            _const_spec((1, d)),
            _const_spec(w_qkv.shape),
            pl.BlockSpec((ng, tm, LANES), lambda j, i: (0, j, 0)),
            pl.BlockSpec((ng, tm, LANES), lambda j, i: (0, j, 0)),
        ],
        out_specs=out_specs,
        out_shape=out_shapes,
        scratch_shapes=[pltpu.VMEM((d // LANES, tm, LANES), F32)],
        compiler_params=_params(2),
        name="qkv_rope",
    )(x, g.reshape(1, d), w_qkv, cos, sin)


def _attn_body(q_ref, k_ref, v_ref, bias_ref, o_ref, st_ref, *, n, residues):
    blk = ATTN_BLOCK
    lane = lax.broadcasted_iota(jnp.int32, (blk, LANES), 1)
    head0 = lane < HEAD_DIM
    stat_head = lane // STAT_LANES
    ones = jnp.ones((2 * blk, LANES), BF16)

    def one_block(ib, carry):
        row0 = pl.multiple_of(ib * blk, blk)
        prow0 = pl.multiple_of(jnp.maximum(ib - 1, 0) * blk, blk)
        bias = bias_ref[jnp.minimum(ib, 1)]
        for r in range(residues):
            one_residue(r, row0, prow0, bias)
        return carry

    def one_residue(r, row0, prow0, bias):
        stats = jnp.zeros((blk, LANES), F32)
        for p in range(GROUP_WIDTH // LANES):
            lanes = slice(r * GROUP_WIDTH + p * LANES, r * GROUP_WIDTH + (p + 1) * LANES)
            q2 = q_ref[0, pl.ds(row0, blk), lanes]
            k2 = jnp.concatenate([k_ref[0, pl.ds(prow0, blk), lanes],
                                  k_ref[0, pl.ds(row0, blk), lanes]], axis=0)
            v2 = jnp.concatenate([v_ref[0, pl.ds(prow0, blk), lanes],
                                  v_ref[0, pl.ds(row0, blk), lanes]], axis=0)
            zero = jnp.zeros_like(q2)
            qq = jnp.concatenate([jnp.where(head0, q2, zero), jnp.where(head0, zero, q2)], axis=0)
            sc = lax.dot_general(qq, k2, (((1,), (1,)), ((), ())), preferred_element_type=F32) + bias
            m = jnp.max(sc, axis=-1, keepdims=True)
            pe = jnp.exp2(sc - m).astype(BF16)
            pvd = _dot(pe, jnp.concatenate([v2, ones], axis=1))
            den = pvd[:, LANES:]
            pv = pvd[:, :LANES] * (1.0 / den)
            o_ref[0, pl.ds(row0, blk), lanes] = jnp.where(head0, pv[:blk], pv[blk:]).astype(BF16)
            lse = m + jnp.log2(den)
            stats = jnp.where(stat_head == 2 * p, lse[:blk], stats)
            stats = jnp.where(stat_head == 2 * p + 1, lse[blk:], stats)
        st_ref[0, r, pl.ds(row0, blk), :] = stats

    lax.fori_loop(0, n // blk, one_block, 0, unroll=min(max(1, ATTN_UNROLL // residues), n // blk))


def _attn_bias():
    blk = ATTN_BLOCK
    qi = np.arange(2 * blk)[:, None] % blk
    kj = np.arange(2 * blk)[None, :]
    band = (kj >= qi) & (kj <= qi + blk)
    first = band & (kj >= blk)
    return jnp.asarray(np.where(np.stack([first, band]), 0.0, NEG_INF), F32)


def _attn_group(qkv, dil):
    b, n, _ = qkv.shape
    residues = min(dil, ATTN_UNROLL)
    steps = dil // residues
    width = residues * GROUP_WIDTH
    seq = pl.BlockSpec((1, n, width), lambda i, r: (i, 0, r))
    kinds = [pl.BlockSpec((1, n, width), functools.partial(lambda i, r, kind: (i, 0, kind * steps + r), kind=kind))
             for kind in range(3)]
    bias = _attn_bias()
    return pl.pallas_call(
        functools.partial(_attn_body, n=n, residues=residues),
        grid=(b, steps),
        in_specs=kinds + [_const_spec(bias.shape)],
        out_specs=[seq, pl.BlockSpec((1, residues, n, LANES), lambda i, r: (i, r, 0, 0))],
        out_shape=[jax.ShapeDtypeStruct((b, n, dil * GROUP_WIDTH), BF16),
                   jax.ShapeDtypeStruct((b, dil, n, LANES), F32)],
        compiler_params=_params(2),
        name=f"attn_dil{dil}",
    )(qkv, qkv, qkv, bias)


def _merge_body(*refs, tm):
    ng = len(ATTN_GROUPS)
    o_refs, st_refs = refs[:ng], refs[ng:2 * ng]
    ex_ref, out_ref, oscr, sscr = refs[2 * ng:]
    n_slabs = GROUP_WIDTH // LANES
    vals, lses = [], []
    for gi, (_, dil) in enumerate(ATTN_GROUPS):
        if dil == 1:
            vals.append(o_refs[gi][0].astype(F32))
            lses.append(st_refs[gi][0, 0])
            continue
        rows = tm // dil
        for r in range(dil):
            for s in range(n_slabs):
                lo = r * GROUP_WIDTH + s * LANES
                oscr[gi, s, pl.ds(r, rows, stride=dil), :] = o_refs[gi][0, :, lo:lo + LANES].astype(F32)
            sscr[gi, pl.ds(r, rows, stride=dil), :] = st_refs[gi][0, r]
        vals.append(jnp.concatenate([oscr[gi, s] for s in range(n_slabs)], axis=-1))
        lses.append(sscr[gi])
    top = functools.reduce(jnp.maximum, lses)
    es = [jnp.exp2(l - top) for l in lses]
    inv = 1.0 / functools.reduce(lambda a, c: a + c, es)
    o = None
    for e, val in zip(es, vals):
        alpha = e * inv
        hi = alpha.astype(BF16)
        lo = (alpha - hi.astype(F32)).astype(BF16)
        wide = _dot(hi, ex_ref[...]) + _dot(lo, ex_ref[...])
        o = wide * val if o is None else o + wide * val
    out_ref[0] = o.astype(BF16)


def _merge_groups(os_, sts):
    b = os_[0].shape[0]
    s = os_[0].shape[1] * os_[0].shape[2] // GROUP_WIDTH
    tm = MERGE_TILE
    ng = len(ATTN_GROUPS)
    src = np.arange(LANES)[:, None]
    dst = np.arange(GROUP_WIDTH)[None, :]
    expand = jnp.asarray(src == (dst // HEAD_DIM) * STAT_LANES, BF16)
    in_specs = []
    for _, dil in ATTN_GROUPS:
        in_specs.append(pl.BlockSpec((1, tm // dil, dil * GROUP_WIDTH), lambda i, j: (i, j, 0)))
    for _, dil in ATTN_GROUPS:
        in_specs.append(pl.BlockSpec((1, dil, tm // dil, LANES), lambda i, j: (i, 0, j, 0)))
    in_specs.append(_const_spec(expand.shape))
    return pl.pallas_call(
        functools.partial(_merge_body, tm=tm),
        grid=(b, s // tm),
        in_specs=in_specs,
        out_specs=pl.BlockSpec((1, tm, GROUP_WIDTH), lambda i, j: (i, j, 0)),
        out_shape=jax.ShapeDtypeStruct((b, s, GROUP_WIDTH), BF16),
        scratch_shapes=[pltpu.VMEM((ng, GROUP_WIDTH // LANES, tm, LANES), F32),
                        pltpu.VMEM((ng, tm, LANES), F32)],
        compiler_params=_params(2),
        name="attn_merge",
    )(*os_, *sts, expand)


def _attn_heads(x, g, w_qkv):
    ng = len(ATTN_GROUPS)
    qkv = _qkv_rope(x, g, w_qkv)
    os_, sts = [], []
    for gi, (_, dil) in enumerate(ATTN_GROUPS):
        o, st = _attn_group(qkv[gi], dil)
        os_.append(o)
        sts.append(st)
    return _merge_groups(os_, sts)


def kernel(x, norm_mix_g, norm_ffn_g, final_norm_g, conv_w_in, conv_b_in, conv_w_dw, conv_b_dw,
           conv_ln_g, conv_ln_b, conv_w_out, conv_b_out, attn_w_qkv, attn_w_o, pool_w, pool_scale,
           ffn_w_gate, ffn_w_up, ffn_w_down):
    b, s, d = x.shape
    depth = norm_mix_g.shape[0]
    bf = lambda w: w.astype(BF16)
    conv_w_in, conv_w_out = bf(conv_w_in), bf(conv_w_out)
    ffn_w_gate, ffn_w_up, ffn_w_down = bf(ffn_w_gate), bf(ffn_w_up), bf(ffn_w_down)
    for i in range(depth):
        kind, j = i % 3, i // 3
        heads = w_o = None
        if kind == 0:
            x = _conv_mixer(x, norm_mix_g[i], conv_w_in, conv_b_in[j], conv_w_dw[j], conv_b_dw[j],
                            conv_ln_g[j], conv_ln_b[j], conv_w_out, conv_b_out[j], layer=j)
        elif kind == 1:
            heads = _attn_heads(x, norm_mix_g[i], bf(attn_w_qkv[j])).reshape(b * s, GROUP_WIDTH)
            w_o = bf(attn_w_o[j])
        else:
            x = _pool_mixer(x, norm_mix_g[i], bf(pool_w[j]), pool_scale[j])
        last = i == depth - 1
        x = _ffn(x.reshape(b * s, d), norm_ffn_g[i], ffn_w_gate, ffn_w_up, ffn_w_down, layer=i,
                 final_g=final_norm_g if last else None, heads=heads, w_o=w_o).reshape(b, s, d)
    return x
```
